```python
import jax, jax.numpy as jnp
from jax import lax
import numpy as np

D_MODEL = 1024
BATCH = 4
SEQ = 4096
DEPTH = 1

MEM_LEN = 256
D_MIX = D_MODEL
POOL_WINDOWS = (2, 4, 8, 16)
N_POOL_GROUPS = len(POOL_WINDOWS)
D_POOL = D_MIX // 4
POOL_GROUP_DIM = D_POOL // N_POOL_GROUPS
D_FOX = D_MIX - D_POOL
FOX_HEAD_DIM = 64
FOX_HEADS = D_FOX // FOX_HEAD_DIM
Q_BLOCK = 128
XA_HEADS = 4
XA_HEAD_DIM = D_MODEL // XA_HEADS
D_FF = 4 * D_MODEL
CONV_WIDTH = 3
NORM_EPS = 1e-6
D_IN = D_POOL + 3 * D_FOX + FOX_HEADS

kernel_name = 'hybrid_pool_fox_memxattn_convffn'


def rms_norm(x, g):
    xf = x.astype(jnp.float32)
    y = xf * lax.rsqrt(jnp.mean(xf * xf, axis=-1, keepdims=True) + NORM_EPS)
    return (y * g.astype(jnp.float32)).astype(x.dtype)


def pool_mixer(u, w_pool, pool_scale):
    b, s, _ = u.shape
    ug = u.astype(jnp.float32).reshape(b, s, N_POOL_GROUPS, POOL_GROUP_DIM)
    csum = jnp.pad(jnp.cumsum(ug, axis=1), ((0, 0), (1, 0), (0, 0), (0, 0)))
    t1 = jnp.arange(1, s + 1, dtype=jnp.float32)
    pooled = []
    for g, w in enumerate(POOL_WINDOWS):
        c = csum[:, :, g]
        lower = jnp.concatenate([jnp.zeros((b, w - 1, POOL_GROUP_DIM), c.dtype), c[:, :s + 1 - w]], axis=1)
        win_sum = c[:, 1:] - lower
        count = jnp.minimum(t1, float(w))[None, :, None]
        pooled.append(win_sum / count)
    pooled = jnp.stack(pooled, axis=2)
    diff = (pooled - ug).astype(u.dtype)
    mixed = jnp.einsum('bsgc,gcd->bsgd', diff, w_pool)
    return (mixed * pool_scale).reshape(b, s, D_POOL)


def forgetting_attention(q, k, v, log_f):
    b, s, h, dh = q.shape
    cum = jnp.cumsum(log_f, axis=1).transpose(0, 2, 1)
    scale = dh ** -0.5
    outs = []
    for i in range(s // Q_BLOCK):
        q0, q1 = i * Q_BLOCK, (i + 1) * Q_BLOCK
        qb, kb, vb = q[:, q0:q1], k[:, :q1], v[:, :q1]
        logits = jnp.einsum('bqhd,bkhd->bhqk', qb, kb).astype(jnp.float32) * scale
        decay = cum[:, :, q0:q1, None] - cum[:, :, None, :q1]
        mask = (q0 + jnp.arange(Q_BLOCK))[:, None] >= jnp.arange(q1)[None, :]
        logits = jnp.where(mask, logits + decay, -jnp.inf)
        probs = jax.nn.softmax(logits, axis=-1)
        outs.append(jnp.einsum('bhqk,bkhd->bqhd', probs.astype(v.dtype), vb))
    return jnp.concatenate(outs, axis=1)


def memory_cross_attention(h, mem_n, w_xq, w_xkv, w_xo):
    b, s, _ = h.shape
    m = mem_n.shape[1]
    q = (h @ w_xq).reshape(b, s, XA_HEADS, XA_HEAD_DIM)
    kv = mem_n @ w_xkv
    k = kv[..., :D_MODEL].reshape(b, m, XA_HEADS, XA_HEAD_DIM)
    v = kv[..., D_MODEL:].reshape(b, m, XA_HEADS, XA_HEAD_DIM)
    logits = jnp.einsum('bshd,bmhd->bhsm', q, k).astype(jnp.float32) * (XA_HEAD_DIM ** -0.5)
    probs = jax.nn.softmax(logits, axis=-1)
    out = jnp.einsum('bhsm,bmhd->bshd', probs.astype(v.dtype), v).reshape(b, s, D_MODEL)
    return out @ w_xo


def conv_gated_mlp(h, w_up, conv_w, conv_b, w_down):
    hid = h @ w_up
    hid = lax.conv_general_dilated(
        hid, conv_w, window_strides=(1,), padding=[(CONV_WIDTH - 1, 0)],
        dimension_numbers=('NWC', 'WIO', 'NWC'), feature_group_count=2 * D_FF) + conv_b
    gate, up = hid[..., :D_FF], hid[..., D_FF:]
    return (jax.nn.gelu(gate, approximate=True) * up) @ w_down


def setup_inputs(seed: int = 0) -> dict:
    key = jax.random.key(seed)
    ks = jax.random.split(key, 22)
    nrm = lambda k, shape, s: jax.random.normal(k, shape, jnp.float32) * s
    gain = lambda k: 1.0 + 0.1 * jax.random.normal(k, (DEPTH, D_MODEL), jnp.float32)
    return {
        'x': jax.random.normal(ks[0], (BATCH, SEQ, D_MODEL), jnp.float32),
        'mem': jax.random.normal(ks[1], (BATCH, MEM_LEN, D_MODEL), jnp.float32),
        'norm_mix_pre': gain(ks[2]),
        'norm_mix_post': gain(ks[3]),
        'w_in': nrm(ks[4], (DEPTH, D_MODEL, D_IN), D_MODEL ** -0.5),
        'b_forget': jax.random.uniform(ks[5], (DEPTH, FOX_HEADS), jnp.float32, 1.0, 6.0),
        'w_pool': nrm(ks[6], (DEPTH, N_POOL_GROUPS, POOL_GROUP_DIM, POOL_GROUP_DIM), POOL_GROUP_DIM ** -0.5),
        'pool_scale': 1.0 + 0.1 * jax.random.normal(ks[7], (DEPTH, N_POOL_GROUPS, POOL_GROUP_DIM), jnp.float32),
        'w_mix_out': nrm(ks[8], (DEPTH, D_MIX, D_MODEL), D_MIX ** -0.5),
        'norm_mem': gain(ks[9]),
        'norm_xa_pre': gain(ks[10]),
        'norm_xa_post': gain(ks[11]),
        'w_xq': nrm(ks[12], (DEPTH, D_MODEL, D_MODEL), D_MODEL ** -0.5),
        'w_xkv': nrm(ks[13], (DEPTH, D_MODEL, 2 * D_MODEL), D_MODEL ** -0.5),
        'w_xo': nrm(ks[14], (DEPTH, D_MODEL, D_MODEL), D_MODEL ** -0.5),
        'norm_ffn_pre': gain(ks[15]),
        'norm_ffn_post': gain(ks[16]),
        'w_up': nrm(ks[17], (DEPTH, D_MODEL, 2 * D_FF), D_MODEL ** -0.5),
        'conv_w': nrm(ks[18], (DEPTH, CONV_WIDTH, 1, 2 * D_FF), CONV_WIDTH ** -0.5),
        'conv_b': nrm(ks[19], (DEPTH, 2 * D_FF), 0.02),
        'w_down': nrm(ks[20], (DEPTH, D_FF, D_MODEL), D_FF ** -0.5),
    }


def reference(x, mem, norm_mix_pre, norm_mix_post, w_in, b_forget, w_pool, pool_scale, w_mix_out,
              norm_mem, norm_xa_pre, norm_xa_post, w_xq, w_xkv, w_xo,
              norm_ffn_pre, norm_ffn_post, w_up, conv_w, conv_b, w_down):
    b, s, _ = x.shape
    for l in range(DEPTH):
        h = rms_norm(x, norm_mix_pre[l])
        proj = h @ w_in[l]
        u_pool = proj[..., :D_POOL]
        o = D_POOL
        q = proj[..., o:o + D_FOX].reshape(b, s, FOX_HEADS, FOX_HEAD_DIM)
        k = proj[..., o + D_FOX:o + 2 * D_FOX].reshape(b, s, FOX_HEADS, FOX_HEAD_DIM)
        v = proj[..., o + 2 * D_FOX:o + 3 * D_FOX].reshape(b, s, FOX_HEADS, FOX_HEAD_DIM)
        f_logit = proj[..., o + 3 * D_FOX:].astype(jnp.float32) + b_forget[l].astype(jnp.float32)
        log_f = jax.nn.log_sigmoid(f_logit)
        y_pool = pool_mixer(u_pool, w_pool[l], pool_scale[l])
        y_fox = forgetting_attention(q, k, v, log_f).reshape(b, s, D_FOX)
        y = jnp.concatenate([y_pool, y_fox], axis=-1) @ w_mix_out[l]
        x = x + rms_norm(y, norm_mix_post[l])
        h = rms_norm(x, norm_xa_pre[l])
        mem_n = rms_norm(mem, norm_mem[l])
        y = memory_cross_attention(h, mem_n, w_xq[l], w_xkv[l], w_xo[l])
        x = x + rms_norm(y, norm_xa_post[l])
        h = rms_norm(x, norm_ffn_pre[l])
        y = conv_gated_mlp(h, w_up[l], conv_w[l], conv_b[l], w_down[l])
        x = x + rms_norm(y, norm_ffn_post[l])
    return x
```

```python
import functools
import math

import jax
import jax.numpy as jnp
from jax import lax
from jax.experimental import pallas as pl
from jax.experimental.pallas import tpu as pltpu

F32 = jnp.float32
BF16 = jnp.bfloat16

NORM_EPS = 1e-6
POOL_WINDOWS = (2, 4, 8, 16)
POOL_GROUP_DIM = 64
D_POOL = 256
FOX_HEAD_DIM = 64
XA_HEADS = 4
CONV_WIDTH = 3
LANES = 128
POOL_HALO = 16
CONV_HALO = 16
AUG = 3
VMEM_LIMIT = 56 * 1024 * 1024


def _rms(x, g):
    ms = jnp.mean(x * x, axis=-1, keepdims=True)
    return x * lax.rsqrt(ms + NORM_EPS) * g


def _const_spec(shape):
    nd = len(shape)
    return pl.BlockSpec(shape, lambda *_: (0,) * nd, pipeline_mode=pl.Buffered(1))


def _params(n_axes):
    return pltpu.CompilerParams(dimension_semantics=("arbitrary",) * n_axes,
                                vmem_limit_bytes=VMEM_LIMIT)


def _in_proj_kernel(x_ref, g_ref, wm_ref, wf_ref, bf_ref,
                    u_ref, qa_ref, ka_ref, v_ref, carry_ref, *, tiles_per_seq, n_heads):
    i = pl.program_id(0)
    tm = x_ref.shape[0]
    d_fox = n_heads * FOX_HEAD_DIM

    @pl.when(i % tiles_per_seq == 0)
    def _():
        carry_ref[...] = jnp.zeros_like(carry_ref)

    h = _rms(x_ref[...], g_ref[...]).astype(BF16)
    main = jnp.dot(h, wm_ref[...], preferred_element_type=F32)
    u_ref[...] = main[:, :D_POOL]
    q = main[:, D_POOL:D_POOL + d_fox] * (FOX_HEAD_DIM ** -0.5)
    k = main[:, D_POOL + d_fox:D_POOL + 2 * d_fox]
    v_ref[...] = main[:, D_POOL + 2 * d_fox:D_POOL + 3 * d_fox].astype(BF16)

    f = jnp.dot(h, wf_ref[...], preferred_element_type=F32) + bf_ref[...]
    c = jnp.minimum(f, 0.0) - jnp.log1p(jnp.exp(-jnp.abs(f)))
    row = lax.broadcasted_iota(jnp.int32, c.shape, 0)
    d = 1
    while d < tm:
        c = c + jnp.where(row >= d, pltpu.roll(c, d, 0), 0.0)
        d *= 2
    c = c + carry_ref[...]
    carry_ref[...] = c[tm - 1:tm, :]

    hi = c.astype(BF16).astype(F32)
    r1 = c - hi
    mid = r1.astype(BF16).astype(F32)
    lo = (r1 - mid).astype(BF16).astype(F32)

    lane = lax.broadcasted_iota(jnp.int32, (tm, LANES), 1)
    for hd in range(n_heads):
        pair, odd = divmod(hd, 2)
        qk_lo = odd * FOX_HEAD_DIM
        a0 = (1 - odd) * FOX_HEAD_DIM
        bc = [jnp.broadcast_to(p[:, hd:hd + 1], (tm, LANES)) for p in (hi, mid, lo)]
        qaug = jnp.zeros((tm, LANES), F32)
        kaug = jnp.zeros((tm, LANES), F32)
        for n in range(AUG):
            qaug = jnp.where(lane == a0 + n, bc[n], qaug)
            kaug = jnp.where(lane == a0 + AUG + n, -bc[n], kaug)
        qaug = jnp.where((lane >= a0 + AUG) & (lane < a0 + 2 * AUG), 1.0, qaug)
        kaug = jnp.where((lane >= a0) & (lane < a0 + AUG), 1.0, kaug)
        in_head = (lane >= qk_lo) & (lane < qk_lo + FOX_HEAD_DIM)
        sl = slice(pair * LANES, (pair + 1) * LANES)
        osl = slice(hd * LANES, (hd + 1) * LANES)
        qa_ref[:, osl] = jnp.where(in_head, q[:, sl], qaug).astype(BF16)
        ka_ref[:, osl] = jnp.where(in_head, k[:, sl], kaug).astype(BF16)


def _in_proj(x2d, g, w_main, w_f, b_f, *, seq, n_heads, tm):
    t, d = x2d.shape
    d_fox = n_heads * FOX_HEAD_DIM
    kern = functools.partial(_in_proj_kernel, tiles_per_seq=seq // tm, n_heads=n_heads)
    row = lambda i: (i, 0)
    return pl.pallas_call(
        kern,
        grid=(t // tm,),
        in_specs=[pl.BlockSpec((tm, d), row), _const_spec(g.shape), _const_spec(w_main.shape),
                  _const_spec(w_f.shape), _const_spec(b_f.shape)],
        out_specs=[pl.BlockSpec((tm, D_POOL), row), pl.BlockSpec((tm, n_heads * LANES), row),
                   pl.BlockSpec((tm, n_heads * LANES), row), pl.BlockSpec((tm, d_fox), row)],
        out_shape=[jax.ShapeDtypeStruct((t, D_POOL), F32),
                   jax.ShapeDtypeStruct((t, n_heads * LANES), BF16),
                   jax.ShapeDtypeStruct((t, n_heads * LANES), BF16),
                   jax.ShapeDtypeStruct((t, d_fox), BF16)],
        scratch_shapes=[pltpu.VMEM((1, LANES), F32)],
        compiler_params=_params(1),
        name="in_proj",
    )(x2d, g, w_main, w_f, b_f)


def _attn_kernel(q_ref, k_ref, v_ref, o_ref, m_ref, l_ref, acc_ref, *, tk):
    i = pl.program_id(2)
    tq = q_ref.shape[0]
    m_ref[...] = jnp.full_like(m_ref, -jnp.inf)
    l_ref[...] = jnp.zeros_like(l_ref)
    acc_ref[...] = jnp.zeros_like(acc_ref)

    def chunk(j, masked):
        ks = pl.multiple_of(j * tk, tk)
        v = v_ref[pl.ds(ks, tk), :]
        for hh in range(2):
            q = q_ref[:, hh * LANES:(hh + 1) * LANES]
            k = k_ref[pl.ds(ks, tk), hh * LANES:(hh + 1) * LANES]
            s = lax.dot_general(q, k, (((1,), (1,)), ((), ())), preferred_element_type=F32)
            if masked:
                r = lax.broadcasted_iota(jnp.int32, (tq, tk), 0)
                cc = lax.broadcasted_iota(jnp.int32, (tq, tk), 1)
                s = jnp.where(r >= cc, s, -jnp.inf)
            m_prev = m_ref[hh]
            m_next = jnp.maximum(m_prev, jnp.max(s, axis=1, keepdims=True))
            alpha = jnp.exp(m_prev - m_next)
            p = jnp.exp(s - m_next)
            l_ref[hh] = alpha * l_ref[hh] + jnp.sum(p, axis=1, keepdims=True)
            m_ref[hh] = m_next
            acc_ref[hh] = alpha * acc_ref[hh] + jnp.dot(p.astype(BF16), v, preferred_element_type=F32)

    def body(j, carry):
        chunk(j, False)
        return carry

    lax.fori_loop(0, i, body, 0)
    chunk(i, True)

    lane = lax.broadcasted_iota(jnp.int32, (tq, LANES), 1)
    o_e = acc_ref[0] / l_ref[0]
    o_o = acc_ref[1] / l_ref[1]
    o_ref[...] = jnp.where(lane < FOX_HEAD_DIM, o_e, o_o).astype(o_ref.dtype)


def _attention(qa, ka, v, *, batch, seq, n_heads, tq):
    t = qa.shape[0]
    n_pairs = n_heads // 2
    nq = seq // tq
    kern = functools.partial(_attn_kernel, tk=tq)
    return pl.pallas_call(
        kern,
        grid=(batch, n_pairs, nq),
        in_specs=[pl.BlockSpec((tq, 2 * LANES), lambda b, p, i: (b * nq + i, p)),
                  pl.BlockSpec((seq, 2 * LANES), lambda b, p, i: (b, p)),
                  pl.BlockSpec((seq, LANES), lambda b, p, i: (b, p))],
        out_specs=pl.BlockSpec((tq, LANES), lambda b, p, i: (b * nq + i, p)),
        out_shape=jax.ShapeDtypeStruct((t, n_pairs * LANES), BF16),
        scratch_shapes=[pltpu.VMEM((2, tq, 1), F32), pltpu.VMEM((2, tq, 1), F32),
                        pltpu.VMEM((2, tq, LANES), F32)],
        compiler_params=_params(3),
        name="fox_attn",
    )(qa, ka, v)


def _mem_kv_kernel(mem_ref, g_ref, w_ref, kv_ref):
    h = _rms(mem_ref[...], g_ref[...]).astype(BF16)
    kv_ref[...] = jnp.dot(h, w_ref[...], preferred_element_type=F32).astype(BF16)


def _mem_kv(mem2d, g, w_xkv, *, batch):
    tmem = mem2d.shape[0] // batch
    d = mem2d.shape[1]
    n = w_xkv.shape[1]
    return pl.pallas_call(
        _mem_kv_kernel,
        grid=(batch,),
        in_specs=[pl.BlockSpec((tmem, d), lambda b: (b, 0)), _const_spec(g.shape),
                  _const_spec(w_xkv.shape)],
        out_specs=pl.BlockSpec((tmem, n), lambda b: (b, 0)),
        out_shape=jax.ShapeDtypeStruct((mem2d.shape[0], n), BF16),
        compiler_params=_params(1),
        name="mem_kv",
    )(mem2d, g, w_xkv)


def _mix_kernel(x_ref, u_ref, up_ref, yf_ref, kv_ref, wp_ref, ps_ref, wmp_ref, wmf_ref,
                g_mpost_ref, g_xpre_ref, wq_ref, wo_ref, g_xpost_ref, o_ref, *, tiles_per_seq):
    i = pl.program_id(0)
    tm, d = x_ref.shape
    seq_tile = i % tiles_per_seq

    halo = jnp.where(seq_tile == 0, 0.0, up_ref[...])
    ub = jnp.concatenate([halo, u_ref[...]], axis=0)
    s2 = ub + pltpu.roll(ub, 1, 0)
    s4 = s2 + pltpu.roll(s2, 2, 0)
    s8 = s4 + pltpu.roll(s4, 4, 0)
    s16 = s8 + pltpu.roll(s8, 8, 0)
    sums = dict(zip(POOL_WINDOWS, (s2, s4, s8, s16)))
    pos = seq_tile * tm + lax.broadcasted_iota(jnp.int32, (tm, D_POOL), 0) + 1
    lane = lax.broadcasted_iota(jnp.int32, (tm, D_POOL), 1)
    pooled = jnp.zeros((tm, D_POOL), F32)
    for g, w in enumerate(POOL_WINDOWS):
        cnt = jnp.minimum(pos, w).astype(F32)
        mean = sums[w][POOL_HALO:, :] / cnt
        in_group = (lane >= g * POOL_GROUP_DIM) & (lane < (g + 1) * POOL_GROUP_DIM)
        pooled = jnp.where(in_group, mean, pooled)
    diff = (pooled - u_ref[...]).astype(BF16)
    y_pool = jnp.dot(diff, wp_ref[...], preferred_element_type=F32) * ps_ref[...]

    y = jnp.dot(y_pool.astype(BF16), wmp_ref[...], preferred_element_type=F32)
    y = y + jnp.dot(yf_ref[...], wmf_ref[...], preferred_element_type=F32)
    x1 = x_ref[...] + _rms(y, g_mpost_ref[...])

    h = _rms(x1, g_xpre_ref[...]).astype(BF16)
    q = jnp.dot(h, wq_ref[...], preferred_element_type=F32)
    dh = d // XA_HEADS
    outs = []
    for hd in range(XA_HEADS):
        qh = (q[:, hd * dh:(hd + 1) * dh] * (dh ** -0.5)).astype(BF16)
        kh = kv_ref[:, hd * dh:(hd + 1) * dh]
        vh = kv_ref[:, d + hd * dh:d + (hd + 1) * dh]
        s = lax.dot_general(qh, kh, (((1,), (1,)), ((), ())), preferred_element_type=F32)
        p = jnp.exp(s - jnp.max(s, axis=1, keepdims=True))
        o = jnp.dot(p.astype(BF16), vh, preferred_element_type=F32)
        outs.append((o / jnp.sum(p, axis=1, keepdims=True)).astype(BF16))
    att = jnp.concatenate(outs, axis=1)
    y2 = jnp.dot(att, wo_ref[...], preferred_element_type=F32)
    o_ref[...] = x1 + _rms(y2, g_xpost_ref[...])


def _mix(x2d, u, yf, kv, w_pool_bd, pool_scale, w_mix_pool, w_mix_fox, g_mpost, g_xpre,
         w_xq, w_xo, g_xpost, *, seq, tm):
    t, d = x2d.shape
    tps = seq // tm
    hb = tm // POOL_HALO
    tmem = kv.shape[0] // (t // seq)
    kern = functools.partial(_mix_kernel, tiles_per_seq=tps)
    row = lambda i: (i, 0)
    consts = (w_pool_bd, pool_scale, w_mix_pool, w_mix_fox, g_mpost, g_xpre, w_xq, w_xo, g_xpost)
    return pl.pallas_call(
        kern,
        grid=(t // tm,),
        in_specs=[pl.BlockSpec((tm, d), row), pl.BlockSpec((tm, D_POOL), row),
                  pl.BlockSpec((POOL_HALO, D_POOL), lambda i: (jnp.maximum(i * hb - 1, 0), 0)),
                  pl.BlockSpec((tm, yf.shape[1]), row),
                  pl.BlockSpec((tmem, kv.shape[1]), lambda i: (i // tps, 0))]
                 + [_const_spec(c.shape) for c in consts],
        out_specs=pl.BlockSpec((tm, d), row),
        out_shape=jax.ShapeDtypeStruct((t, d), F32),
        compiler_params=_params(1),
        name="mix_xattn",
    )(x2d, u, u, yf, kv, *consts)


def _gelu_tanh(x):
    return 0.5 * x * (1.0 + jnp.tanh(math.sqrt(2.0 / math.pi) * (x + 0.044715 * (x * x * x))))


def _ffn_kernel(x_ref, xp_ref, g_pre_ref, wg_ref, wu_ref, cwg_ref, cwu_ref, cbg_ref, cbu_ref,
                wd_ref, g_post_ref, o_ref, h_ref, acc_ref, *, tiles_per_seq):
    i = pl.program_id(0)
    c = pl.program_id(1)
    tm = x_ref.shape[0]

    @pl.when(c == 0)
    def _():
        prev = jnp.where(i % tiles_per_seq == 0, 0.0, xp_ref[...])
        xs = jnp.concatenate([prev, x_ref[...]], axis=0)
        h_ref[...] = _rms(xs, g_pre_ref[...]).astype(BF16)
        acc_ref[...] = jnp.zeros_like(acc_ref)

    def conv(w_ref, cw_ref, cb_ref):
        hid = jnp.dot(h_ref[...], w_ref[...], preferred_element_type=F32)
        out = hid * cw_ref[CONV_WIDTH - 1:CONV_WIDTH, :]
        for tap in range(1, CONV_WIDTH):
            out = out + pltpu.roll(hid, tap, 0) * cw_ref[CONV_WIDTH - 1 - tap:CONV_WIDTH - tap, :]
        return out[CONV_HALO:, :] + cb_ref[...]

    gate = conv(wg_ref, cwg_ref, cbg_ref)
    up = conv(wu_ref, cwu_ref, cbu_ref)
    act = (_gelu_tanh(gate) * up).astype(BF16)
    acc_ref[...] += jnp.dot(act, wd_ref[...], preferred_element_type=F32)

    @pl.when(c == pl.num_programs(1) - 1)
    def _():
        o_ref[...] = x_ref[...] + _rms(acc_ref[...], g_post_ref[...])


def _ffn(x2d, g_pre, w_up, conv_w, conv_b, w_down, g_post, *, seq, tm, ck):
    t, d = x2d.shape
    d_ff = w_down.shape[0]
    n_ck = d_ff // ck
    hb = tm // CONV_HALO
    kern = functools.partial(_ffn_kernel, tiles_per_seq=seq // tm)
    row = lambda i, c: (i, 0)
    return pl.pallas_call(
        kern,
        grid=(t // tm, n_ck),
        in_specs=[pl.BlockSpec((tm, d), row),
                  pl.BlockSpec((CONV_HALO, d), lambda i, c: (jnp.maximum(i * hb - 1, 0), 0)),
                  _const_spec(g_pre.shape),
                  pl.BlockSpec((d, ck), lambda i, c: (0, c)),
                  pl.BlockSpec((d, ck), lambda i, c: (0, c + n_ck)),
                  pl.BlockSpec((CONV_WIDTH, ck), lambda i, c: (0, c)),
                  pl.BlockSpec((CONV_WIDTH, ck), lambda i, c: (0, c + n_ck)),
                  pl.BlockSpec((1, ck), lambda i, c: (0, c)),
                  pl.BlockSpec((1, ck), lambda i, c: (0, c + n_ck)),
                  pl.BlockSpec((ck, d), lambda i, c: (c, 0)),
                  _const_spec(g_post.shape)],
        out_specs=pl.BlockSpec((tm, d), row),
        out_shape=jax.ShapeDtypeStruct((t, d), F32),
        scratch_shapes=[pltpu.VMEM((CONV_HALO + tm, d), BF16), pltpu.VMEM((tm, d), F32)],
        compiler_params=_params(2),
        name="conv_ffn",
    )(x2d, x2d, g_pre, w_up, w_up, conv_w, conv_w, conv_b, conv_b, w_down, g_post)


def _tile(n, pref):
    t = min(n, pref)
    assert n % t == 0, (n, t)
    return t


def kernel(x, mem, norm_mix_pre, norm_mix_post, w_in, b_forget, w_pool, pool_scale, w_mix_out,
           norm_mem, norm_xa_pre, norm_xa_post, w_xq, w_xkv, w_xo,
           norm_ffn_pre, norm_ffn_post, w_up, conv_w, conv_b, w_down):
    b, s, d = x.shape
    depth = w_in.shape[0]
    n_heads = b_forget.shape[1]
    d_fox = n_heads * FOX_HEAD_DIM
    d_main = D_POOL + 3 * d_fox
    assert n_heads % 2 == 0 and n_heads <= LANES
    assert w_in.shape[2] == d_main + n_heads
    tm = _tile(s, 512)
    tq = _tile(s, 512)
    ck = _tile(w_down.shape[1], 1024)

    x2d = x.reshape(b * s, d)
    mem2d = mem.reshape(b * mem.shape[1], d)
    for l in range(depth):
        row = lambda a: a[l].reshape(1, -1)
        w_main = w_in[l, :, :d_main].astype(BF16)
        w_f = jnp.pad(w_in[l, :, d_main:], ((0, 0), (0, LANES - n_heads))).astype(BF16)
        b_f = jnp.pad(b_forget[l], (0, LANES - n_heads)).reshape(1, LANES)
        w_pool_bd = jax.scipy.linalg.block_diag(*w_pool[l]).astype(BF16)
        w_mix = w_mix_out[l].astype(BF16)

        u, qa, ka, v = _in_proj(x2d, row(norm_mix_pre), w_main, w_f, b_f,
                                seq=s, n_heads=n_heads, tm=tm)
        yf = _attention(qa, ka, v, batch=b, seq=s, n_heads=n_heads, tq=tq)
        kv = _mem_kv(mem2d, row(norm_mem), w_xkv[l].astype(BF16), batch=b)
        x2d = _mix(x2d, u, yf, kv, w_pool_bd, pool_scale[l].reshape(1, -1),
                   w_mix[:D_POOL], w_mix[D_POOL:], row(norm_mix_post), row(norm_xa_pre),
                   w_xq[l].astype(BF16), w_xo[l].astype(BF16), row(norm_xa_post), seq=s, tm=tm)
        x2d = _ffn(x2d, row(norm_ffn_pre), w_up[l].astype(BF16), conv_w[l].reshape(CONV_WIDTH, -1),
                   conv_b[l].reshape(1, -1), w_down[l].astype(BF16), row(norm_ffn_post),
                   seq=s, tm=tm, ck=ck)
    return x2d.reshape(b, s, d)
```

```python
import functools
import math

import jax
import jax.numpy as jnp
from jax import lax
from jax.experimental import pallas as pl
from jax.experimental.pallas import tpu as pltpu

F32 = jnp.float32
BF16 = jnp.bfloat16

NORM_EPS = 1e-6
POOL_WINDOWS = (2, 4, 8, 16)
POOL_GROUP_DIM = 64
D_POOL = 256
FOX_HEAD_DIM = 64
XA_HEADS = 4
CONV_WIDTH = 3
LANES = 128
POOL_HALO = 16
CONV_HALO = 16
AUG = 3
VMEM_LIMIT = 56 * 1024 * 1024
LOG2E = math.log2(math.e)


def _rms(x, g):
    ms = jnp.mean(x * x, axis=-1, keepdims=True)
    return x * lax.rsqrt(ms + NORM_EPS) * g


def _const_spec(shape):
    nd = len(shape)
    return pl.BlockSpec(shape, lambda *_: (0,) * nd, pipeline_mode=pl.Buffered(1))


def _params(n_axes):
    return pltpu.CompilerParams(dimension_semantics=("arbitrary",) * n_axes,
                                vmem_limit_bytes=VMEM_LIMIT)


def _in_proj_kernel(x_ref, g_ref, wm_ref, wf_ref, bf_ref,
                    u_ref, qa_ref, ka_ref, v_ref, carry_ref, *, tiles_per_seq, n_heads):
    i = pl.program_id(0)
    tm = x_ref.shape[0]
    d_fox = n_heads * FOX_HEAD_DIM

    @pl.when(i % tiles_per_seq == 0)
    def _():
        carry_ref[...] = jnp.zeros_like(carry_ref)

    h = _rms(x_ref[...], g_ref[...]).astype(BF16)
    main = jnp.dot(h, wm_ref[...], preferred_element_type=F32)
    u_ref[...] = main[:, :D_POOL]
    q = main[:, D_POOL:D_POOL + d_fox] * (FOX_HEAD_DIM ** -0.5 * LOG2E)
    k = main[:, D_POOL + d_fox:D_POOL + 2 * d_fox]
    v_ref[...] = main[:, D_POOL + 2 * d_fox:D_POOL + 3 * d_fox].astype(BF16)

    f = jnp.dot(h, wf_ref[...], preferred_element_type=F32) + bf_ref[...]
    c = jnp.minimum(f, 0.0) - jnp.log1p(jnp.exp(-jnp.abs(f)))
    row = lax.broadcasted_iota(jnp.int32, c.shape, 0)
    d = 1
    while d < tm:
        c = c + jnp.where(row >= d, pltpu.roll(c, d, 0), 0.0)
        d *= 2
    c = c + carry_ref[...]
    carry_ref[...] = c[tm - 1:tm, :]

    c2 = c * LOG2E
    hi = c2.astype(BF16).astype(F32)
    r1 = c2 - hi
    mid = r1.astype(BF16).astype(F32)
    lo = (r1 - mid).astype(BF16).astype(F32)

    lane = lax.broadcasted_iota(jnp.int32, (tm, LANES), 1)
    for hd in range(n_heads):
        pair, odd = divmod(hd, 2)
        qk_lo = odd * FOX_HEAD_DIM
        a0 = (1 - odd) * FOX_HEAD_DIM
        bc = [jnp.broadcast_to(p[:, hd:hd + 1], (tm, LANES)) for p in (hi, mid, lo)]
        qaug = jnp.zeros((tm, LANES), F32)
        kaug = jnp.zeros((tm, LANES), F32)
        for n in range(AUG):
            qaug = jnp.where(lane == a0 + n, bc[n], qaug)
            kaug = jnp.where(lane == a0 + AUG + n, -bc[n], kaug)
        qaug = jnp.where((lane >= a0 + AUG) & (lane < a0 + 2 * AUG), 1.0, qaug)
        kaug = jnp.where((lane >= a0) & (lane < a0 + AUG), 1.0, kaug)
        in_head = (lane >= qk_lo) & (lane < qk_lo + FOX_HEAD_DIM)
        sl = slice(pair * LANES, (pair + 1) * LANES)
        osl = slice(hd * LANES, (hd + 1) * LANES)
        qa_ref[:, osl] = jnp.where(in_head, q[:, sl], qaug).astype(BF16)
        ka_ref[:, osl] = jnp.where(in_head, k[:, sl], kaug).astype(BF16)


def _in_proj(x2d, g, w_main, w_f, b_f, *, seq, n_heads, tm):
    t, d = x2d.shape
    d_fox = n_heads * FOX_HEAD_DIM
    kern = functools.partial(_in_proj_kernel, tiles_per_seq=seq // tm, n_heads=n_heads)
    row = lambda i: (i, 0)
    return pl.pallas_call(
        kern,
        grid=(t // tm,),
        in_specs=[pl.BlockSpec((tm, d), row), _const_spec(g.shape), _const_spec(w_main.shape),
                  _const_spec(w_f.shape), _const_spec(b_f.shape)],
        out_specs=[pl.BlockSpec((tm, D_POOL), row), pl.BlockSpec((tm, n_heads * LANES), row),
                   pl.BlockSpec((tm, n_heads * LANES), row), pl.BlockSpec((tm, d_fox), row)],
        out_shape=[jax.ShapeDtypeStruct((t, D_POOL), F32),
                   jax.ShapeDtypeStruct((t, n_heads * LANES), BF16),
                   jax.ShapeDtypeStruct((t, n_heads * LANES), BF16),
                   jax.ShapeDtypeStruct((t, d_fox), BF16)],
        scratch_shapes=[pltpu.VMEM((1, LANES), F32)],
        compiler_params=_params(1),
        name="in_proj",
    )(x2d, g, w_main, w_f, b_f)


def _attn_kernel(q_ref, k_ref, v_ref, o_ref, m_ref, acc_ref, vtop_ref, vbot_ref, s_ref, *, tk):
    i = pl.program_id(2)
    tq = q_ref.shape[0]
    n_sub = tk // LANES

    @pl.when(i == 0)
    def _():
        v = v_ref[...].astype(F32)
        lane = lax.broadcasted_iota(jnp.int32, v.shape, 1)
        vtop_ref[:, :LANES] = jnp.where(lane < FOX_HEAD_DIM, v, 0.0).astype(BF16)
        vtop_ref[:, LANES:] = jnp.where(lane == 0, 1.0, 0.0).astype(BF16)
        vbot_ref[:, :LANES] = jnp.where(lane >= FOX_HEAD_DIM, v, 0.0).astype(BF16)
        vbot_ref[:, LANES:] = jnp.where(lane == 1, 1.0, 0.0).astype(BF16)

    m_ref[...] = jnp.full_like(m_ref, -jnp.inf)
    acc_ref[...] = jnp.zeros_like(acc_ref)
    lane2 = lax.broadcasted_iota(jnp.int32, (tq, 2 * LANES), 1)
    even_lanes = (lane2 < FOX_HEAD_DIM) | (lane2 == LANES)

    def logits(j, slot):
        ks = pl.multiple_of(j * tk, tk)
        for hh in range(2):
            q = q_ref[:, hh * LANES:(hh + 1) * LANES]
            k = k_ref[pl.ds(ks, tk), hh * LANES:(hh + 1) * LANES]
            s_ref[slot, hh] = lax.dot_general(q, k, (((1,), (1,)), ((), ())),
                                              preferred_element_type=F32)

    def update(j, slot, masked):
        ks = pl.multiple_of(j * tk, tk)
        ps, alphas = [], []
        for hh in range(2):
            s = s_ref[slot, hh]
            if masked:
                r = lax.broadcasted_iota(jnp.int32, (tq, tk), 0)
                cc = lax.broadcasted_iota(jnp.int32, (tq, tk), 1)
                s = jnp.where(r >= cc, s, -jnp.inf)
            m_prev = m_ref[hh]
            m_next = jnp.maximum(m_prev, jnp.max(s, axis=1, keepdims=True))
            alphas.append(jnp.exp2(m_prev - m_next))
            ps.append(jnp.concatenate(
                [jnp.exp2(s[:, c * LANES:(c + 1) * LANES] - m_next) for c in range(n_sub)],
                axis=1).astype(BF16))
            m_ref[hh] = m_next
        pv = jnp.dot(ps[0], vtop_ref[pl.ds(ks, tk), :], preferred_element_type=F32)
        pv = pv + jnp.dot(ps[1], vbot_ref[pl.ds(ks, tk), :], preferred_element_type=F32)
        alpha = jnp.where(even_lanes, jnp.concatenate([alphas[0]] * 2, axis=1),
                          jnp.concatenate([alphas[1]] * 2, axis=1))
        acc_ref[...] = alpha * acc_ref[...] + pv

    logits(i, 0)

    @pl.when(i == 0)
    def _():
        update(i, 0, True)

    @pl.when(i > 0)
    def _():
        logits(i - 1, 1)
        update(i, 0, True)

        def step(c, slot):
            logits(c - 1, 1 - slot)
            update(c, slot, False)

        def two_steps(tt, carry):
            c = i - 1 - 2 * tt
            step(c, 1)
            step(c - 1, 0)
            return carry

        lax.fori_loop(0, lax.shift_right_logical(i - 1, 1), two_steps, 0)

        @pl.when(i % 2 == 0)
        def _():
            step(1, 1)
            update(0, 0, False)

        @pl.when(i % 2 == 1)
        def _():
            update(0, 1, False)

    acc = acc_ref[...]
    l_e = jnp.broadcast_to(acc[:, LANES:LANES + 1], (tq, LANES))
    l_o = jnp.broadcast_to(acc[:, LANES + 1:LANES + 2], (tq, LANES))
    lane = lax.broadcasted_iota(jnp.int32, (tq, LANES), 1)
    o_ref[...] = (acc[:, :LANES] / jnp.where(lane < FOX_HEAD_DIM, l_e, l_o)).astype(o_ref.dtype)


def _attention(qa, ka, v, *, batch, seq, n_heads, tq):
    t = qa.shape[0]
    n_pairs = n_heads // 2
    nq = seq // tq
    kern = functools.partial(_attn_kernel, tk=tq)
    return pl.pallas_call(
        kern,
        grid=(batch, n_pairs, nq),
        in_specs=[pl.BlockSpec((tq, 2 * LANES), lambda b, p, i: (b * nq + i, p)),
                  pl.BlockSpec((seq, 2 * LANES), lambda b, p, i: (b, p)),
                  pl.BlockSpec((seq, LANES), lambda b, p, i: (b, p))],
        out_specs=pl.BlockSpec((tq, LANES), lambda b, p, i: (b * nq + i, p)),
        out_shape=jax.ShapeDtypeStruct((t, n_pairs * LANES), BF16),
        scratch_shapes=[pltpu.VMEM((2, tq, LANES), F32), pltpu.VMEM((tq, 2 * LANES), F32),
                        pltpu.VMEM((seq, 2 * LANES), BF16), pltpu.VMEM((seq, 2 * LANES), BF16),
                        pltpu.VMEM((2, 2, tq, tq), F32)],
        compiler_params=_params(3),
        name="fox_attn",
    )(qa, ka, v)


def _mem_kv_kernel(mem_ref, g_ref, w_ref, kv_ref):
    h = _rms(mem_ref[...], g_ref[...]).astype(BF16)
    kv_ref[...] = jnp.dot(h, w_ref[...], preferred_element_type=F32).astype(BF16)


def _mem_kv(mem2d, g, w_xkv, *, batch):
    tmem = mem2d.shape[0] // batch
    d = mem2d.shape[1]
    n = w_xkv.shape[1]
    return pl.pallas_call(
        _mem_kv_kernel,
        grid=(batch,),
        in_specs=[pl.BlockSpec((tmem, d), lambda b: (b, 0)), _const_spec(g.shape),
                  _const_spec(w_xkv.shape)],
        out_specs=pl.BlockSpec((tmem, n), lambda b: (b, 0)),
        out_shape=jax.ShapeDtypeStruct((mem2d.shape[0], n), BF16),
        compiler_params=_params(1),
        name="mem_kv",
    )(mem2d, g, w_xkv)


def _mix_kernel(x_ref, u_ref, up_ref, yf_ref, kv_ref, wp_ref, ps_ref, wmp_ref, wmf_ref,
                g_mpost_ref, g_xpre_ref, wq_ref, wo_ref, g_xpost_ref, o_ref, *, tiles_per_seq):
    i = pl.program_id(0)
    tm, d = x_ref.shape
    seq_tile = i % tiles_per_seq

    halo = jnp.where(seq_tile == 0, 0.0, up_ref[...])
    ub = jnp.concatenate([halo, u_ref[...]], axis=0)
    s2 = ub + pltpu.roll(ub, 1, 0)
    s4 = s2 + pltpu.roll(s2, 2, 0)
    s8 = s4 + pltpu.roll(s4, 4, 0)
    s16 = s8 + pltpu.roll(s8, 8, 0)
    sums = dict(zip(POOL_WINDOWS, (s2, s4, s8, s16)))
    pos = seq_tile * tm + lax.broadcasted_iota(jnp.int32, (tm, D_POOL), 0) + 1
    lane = lax.broadcasted_iota(jnp.int32, (tm, D_POOL), 1)
    pooled = jnp.zeros((tm, D_POOL), F32)
    for g, w in enumerate(POOL_WINDOWS):
        cnt = jnp.minimum(pos, w).astype(F32)
        mean = sums[w][POOL_HALO:, :] / cnt
        in_group = (lane >= g * POOL_GROUP_DIM) & (lane < (g + 1) * POOL_GROUP_DIM)
        pooled = jnp.where(in_group, mean, pooled)
    diff = (pooled - u_ref[...]).astype(BF16)
    y_pool = jnp.dot(diff, wp_ref[...], preferred_element_type=F32) * ps_ref[...]

    y = jnp.dot(y_pool.astype(BF16), wmp_ref[...], preferred_element_type=F32)
    y = y + jnp.dot(yf_ref[...], wmf_ref[...], preferred_element_type=F32)
    x1 = x_ref[...] + _rms(y, g_mpost_ref[...])

    h = _rms(x1, g_xpre_ref[...]).astype(BF16)
    q = jnp.dot(h, wq_ref[...], preferred_element_type=F32)
    dh = d // XA_HEADS
    outs = []
    for hd in range(XA_HEADS):
        qh = (q[:, hd * dh:(hd + 1) * dh] * (dh ** -0.5)).astype(BF16)
        kh = kv_ref[:, hd * dh:(hd + 1) * dh]
        vh = kv_ref[:, d + hd * dh:d + (hd + 1) * dh]
        s = lax.dot_general(qh, kh, (((1,), (1,)), ((), ())), preferred_element_type=F32)
        p = jnp.exp(s - jnp.max(s, axis=1, keepdims=True))
        o = jnp.dot(p.astype(BF16), vh, preferred_element_type=F32)
        outs.append((o / jnp.sum(p, axis=1, keepdims=True)).astype(BF16))
    att = jnp.concatenate(outs, axis=1)
    y2 = jnp.dot(att, wo_ref[...], preferred_element_type=F32)
    o_ref[...] = x1 + _rms(y2, g_xpost_ref[...])


def _mix(x2d, u, yf, kv, w_pool_bd, pool_scale, w_mix_pool, w_mix_fox, g_mpost, g_xpre,
         w_xq, w_xo, g_xpost, *, seq, tm):
    t, d = x2d.shape
    tps = seq // tm
    hb = tm // POOL_HALO
    tmem = kv.shape[0] // (t // seq)
    kern = functools.partial(_mix_kernel, tiles_per_seq=tps)
    row = lambda i: (i, 0)
    consts = (w_pool_bd, pool_scale, w_mix_pool, w_mix_fox, g_mpost, g_xpre, w_xq, w_xo, g_xpost)
    return pl.pallas_call(
        kern,
        grid=(t // tm,),
        in_specs=[pl.BlockSpec((tm, d), row), pl.BlockSpec((tm, D_POOL), row),
                  pl.BlockSpec((POOL_HALO, D_POOL), lambda i: (jnp.maximum(i * hb - 1, 0), 0)),
                  pl.BlockSpec((tm, yf.shape[1]), row),
                  pl.BlockSpec((tmem, kv.shape[1]), lambda i: (i // tps, 0))]
                 + [_const_spec(c.shape) for c in consts],
        out_specs=pl.BlockSpec((tm, d), row),
        out_shape=jax.ShapeDtypeStruct((t, d), F32),
        compiler_params=_params(1),
        name="mix_xattn",
    )(x2d, u, u, yf, kv, *consts)


def _gelu_tanh(x):
    return 0.5 * x * (1.0 + jnp.tanh(math.sqrt(2.0 / math.pi) * (x + 0.044715 * (x * x * x))))


def _ffn_kernel(x_ref, xp_ref, g_pre_ref, wg_ref, wu_ref, cwg_ref, cwu_ref, cbg_ref, cbu_ref,
                wd_ref, g_post_ref, o_ref, h_ref, acc_ref, *, tiles_per_seq):
    i = pl.program_id(0)
    c = pl.program_id(1)
    tm = x_ref.shape[0]

    @pl.when(c == 0)
    def _():
        prev = jnp.where(i % tiles_per_seq == 0, 0.0, xp_ref[...])
        xs = jnp.concatenate([prev, x_ref[...]], axis=0)
        h_ref[...] = _rms(xs, g_pre_ref[...]).astype(BF16)
        acc_ref[...] = jnp.zeros_like(acc_ref)

    def conv(w_ref, cw_ref, cb_ref):
        hid = jnp.dot(h_ref[...], w_ref[...], preferred_element_type=F32)
        out = hid * cw_ref[CONV_WIDTH - 1:CONV_WIDTH, :]
        for tap in range(1, CONV_WIDTH):
            out = out + pltpu.roll(hid, tap, 0) * cw_ref[CONV_WIDTH - 1 - tap:CONV_WIDTH - tap, :]
        return out[CONV_HALO:, :] + cb_ref[...]

    gate = conv(wg_ref, cwg_ref, cbg_ref)
    up = conv(wu_ref, cwu_ref, cbu_ref)
    act = (_gelu_tanh(gate) * up).astype(BF16)
    acc_ref[...] += jnp.dot(act, wd_ref[...], preferred_element_type=F32)

    @pl.when(c == pl.num_programs(1) - 1)
    def _():
        o_ref[...] = x_ref[...] + _rms(acc_ref[...], g_post_ref[...])


def _ffn(x2d, g_pre, w_up, conv_w, conv_b, w_down, g_post, *, seq, tm, ck):
    t, d = x2d.shape
    d_ff = w_down.shape[0]
    n_ck = d_ff // ck
    hb = tm // CONV_HALO
    kern = functools.partial(_ffn_kernel, tiles_per_seq=seq // tm)
    row = lambda i, c: (i, 0)
    return pl.pallas_call(
        kern,
        grid=(t // tm, n_ck),
        in_specs=[pl.BlockSpec((tm, d), row),
                  pl.BlockSpec((CONV_HALO, d), lambda i, c: (jnp.maximum(i * hb - 1, 0), 0)),
                  _const_spec(g_pre.shape),
                  pl.BlockSpec((d, ck), lambda i, c: (0, c)),
                  pl.BlockSpec((d, ck), lambda i, c: (0, c + n_ck)),
                  pl.BlockSpec((CONV_WIDTH, ck), lambda i, c: (0, c)),
                  pl.BlockSpec((CONV_WIDTH, ck), lambda i, c: (0, c + n_ck)),
                  pl.BlockSpec((1, ck), lambda i, c: (0, c)),
                  pl.BlockSpec((1, ck), lambda i, c: (0, c + n_ck)),
                  pl.BlockSpec((ck, d), lambda i, c: (c, 0)),
                  _const_spec(g_post.shape)],
        out_specs=pl.BlockSpec((tm, d), row),
        out_shape=jax.ShapeDtypeStruct((t, d), F32),
        scratch_shapes=[pltpu.VMEM((CONV_HALO + tm, d), BF16), pltpu.VMEM((tm, d), F32)],
        compiler_params=_params(2),
        name="conv_ffn",
    )(x2d, x2d, g_pre, w_up, w_up, conv_w, conv_w, conv_b, conv_b, w_down, g_post)


def _tile(n, pref):
    t = min(n, pref)
    assert n % t == 0, (n, t)
    return t


def kernel(x, mem, norm_mix_pre, norm_mix_post, w_in, b_forget, w_pool, pool_scale, w_mix_out,
           norm_mem, norm_xa_pre, norm_xa_post, w_xq, w_xkv, w_xo,
           norm_ffn_pre, norm_ffn_post, w_up, conv_w, conv_b, w_down):
    b, s, d = x.shape
    depth = w_in.shape[0]
    n_heads = b_forget.shape[1]
    d_fox = n_heads * FOX_HEAD_DIM
    d_main = D_POOL + 3 * d_fox
    assert n_heads % 2 == 0 and n_heads <= LANES
    assert w_in.shape[2] == d_main + n_heads
    tm = _tile(s, 512)
    tq = _tile(s, 512)
    ck = _tile(w_down.shape[1], 1024)

    x2d = x.reshape(b * s, d)
    mem2d = mem.reshape(b * mem.shape[1], d)
    for l in range(depth):
        row = lambda a: a[l].reshape(1, -1)
        w_main = w_in[l, :, :d_main].astype(BF16)
        w_f = jnp.pad(w_in[l, :, d_main:], ((0, 0), (0, LANES - n_heads))).astype(BF16)
        b_f = jnp.pad(b_forget[l], (0, LANES - n_heads)).reshape(1, LANES)
        w_pool_bd = jax.scipy.linalg.block_diag(*w_pool[l]).astype(BF16)
        w_mix = w_mix_out[l].astype(BF16)

        u, qa, ka, v = _in_proj(x2d, row(norm_mix_pre), w_main, w_f, b_f,
                                seq=s, n_heads=n_heads, tm=tm)
        yf = _attention(qa, ka, v, batch=b, seq=s, n_heads=n_heads, tq=tq)
        kv = _mem_kv(mem2d, row(norm_mem), w_xkv[l].astype(BF16), batch=b)
        x2d = _mix(x2d, u, yf, kv, w_pool_bd, pool_scale[l].reshape(1, -1),
                   w_mix[:D_POOL], w_mix[D_POOL:], row(norm_mix_post), row(norm_xa_pre),
                   w_xq[l].astype(BF16), w_xo[l].astype(BF16), row(norm_xa_post), seq=s, tm=tm)
        x2d = _ffn(x2d, row(norm_ffn_pre), w_up[l].astype(BF16), conv_w[l].reshape(CONV_WIDTH, -1),
                   conv_b[l].reshape(1, -1), w_down[l].astype(BF16), row(norm_ffn_post),
                   seq=s, tm=tm, ck=ck)
    return x2d.reshape(b, s, d)
```

```python
import functools
import math

import jax
import jax.numpy as jnp
import numpy as np
from jax import lax
from jax.experimental import pallas as pl
from jax.experimental.pallas import tpu as pltpu

F32 = jnp.float32
BF16 = jnp.bfloat16

NORM_EPS = 1e-6
POOL_WINDOWS = (2, 4, 8, 16)
POOL_GROUP_DIM = 64
D_POOL = 256
FOX_HEAD_DIM = 64
XA_HEADS = 4
CONV_WIDTH = 3
LANES = 128
POOL_HALO = 16
CONV_CARRY = 8
AUG = 3
VMEM_LIMIT = 56 * 1024 * 1024
LOG2E = math.log2(math.e)


def _rms(x, g):
    ms = jnp.mean(x * x, axis=-1, keepdims=True)
    return x * lax.rsqrt(ms + NORM_EPS) * g


def _const_spec(shape):
    nd = len(shape)
    return pl.BlockSpec(shape, lambda *_: (0,) * nd, pipeline_mode=pl.Buffered(1))


def _params(n_axes):
    return pltpu.CompilerParams(dimension_semantics=("arbitrary",) * n_axes,
                                vmem_limit_bytes=VMEM_LIMIT)


def _aug_constants(n_heads):
    place = np.zeros((AUG * LANES, 2 * LANES), np.float32)
    ones = np.zeros((8, 2 * LANES), np.float32)
    for h in range(n_heads):
        for n in range(AUG):
            place[n * LANES + h, 2 * AUG * h + n] = 1.0
            place[n * LANES + h, LANES + 2 * AUG * h + AUG + n] = -1.0
            ones[0, 2 * AUG * h + AUG + n] = 1.0
            ones[0, LANES + 2 * AUG * h + n] = 1.0
    return jnp.asarray(place, BF16), jnp.asarray(ones)


def _in_proj_kernel(x_ref, g_ref, w_ref, bf_ref, place_ref, ones_ref,
                    u_ref, qa_ref, ka_ref, v_ref, carry_ref, wm_ref, wf_ref,
                    *, tiles_per_seq, n_heads):
    i = pl.program_id(0)
    tm = x_ref.shape[0]
    d_fox = n_heads * FOX_HEAD_DIM
    d_main = D_POOL + 3 * d_fox

    @pl.when(i == 0)
    def _():
        wm_ref[...] = w_ref[:, :d_main].astype(BF16)
        wf_ref[...] = jnp.zeros_like(wf_ref)
        wf_ref[:, :n_heads] = w_ref[:, d_main:d_main + n_heads].astype(BF16)

    @pl.when(i % tiles_per_seq == 0)
    def _():
        carry_ref[...] = jnp.zeros_like(carry_ref)

    h = _rms(x_ref[...], g_ref[...]).astype(BF16)

    f = jnp.dot(h, wf_ref[...], preferred_element_type=F32) + bf_ref[...]
    c = jnp.minimum(f, 0.0) - jnp.log1p(jnp.exp(-jnp.abs(f)))
    row = lax.broadcasted_iota(jnp.int32, c.shape, 0)
    d = 1
    while d < tm:
        c = c + jnp.where(row >= d, pltpu.roll(c, d, 0), 0.0)
        d *= 2
    c = c + carry_ref[...]
    carry_ref[...] = c[tm - 1:tm, :]

    c2 = c * LOG2E
    hi = c2.astype(BF16)
    r1 = c2 - hi.astype(F32)
    mid = r1.astype(BF16)
    lo = (r1 - mid.astype(F32)).astype(BF16)
    pieces = jnp.concatenate([hi, mid, lo], axis=1)
    aug = jnp.dot(pieces, place_ref[...], preferred_element_type=F32) + ones_ref[0:1, :]
    aug_q, aug_k = aug[:, :LANES], aug[:, LANES:]

    main = jnp.dot(h, wm_ref[...], preferred_element_type=F32)
    u_ref[...] = main[:, :D_POOL]
    q = main[:, D_POOL:D_POOL + d_fox] * (FOX_HEAD_DIM ** -0.5 * LOG2E)
    k = main[:, D_POOL + d_fox:D_POOL + 2 * d_fox]
    v_ref[...] = main[:, D_POOL + 2 * d_fox:D_POOL + 3 * d_fox].astype(BF16)

    lane = lax.broadcasted_iota(jnp.int32, (tm, LANES), 1)
    for hd in range(n_heads):
        pair, odd = divmod(hd, 2)
        qk_lo = odd * FOX_HEAD_DIM
        a0 = (1 - odd) * FOX_HEAD_DIM
        shift = (a0 - 2 * AUG * hd) % LANES
        in_head = (lane >= qk_lo) & (lane < qk_lo + FOX_HEAD_DIM)
        in_aug = (lane >= a0) & (lane < a0 + 2 * AUG)
        sl = slice(pair * LANES, (pair + 1) * LANES)
        osl = slice(hd * LANES, (hd + 1) * LANES)
        rq = jnp.where(in_aug, pltpu.roll(aug_q, shift, 1), 0.0)
        rk = jnp.where(in_aug, pltpu.roll(aug_k, shift, 1), 0.0)
        qa_ref[:, osl] = jnp.where(in_head, q[:, sl], rq).astype(BF16)
        ka_ref[:, osl] = jnp.where(in_head, k[:, sl], rk).astype(BF16)


def _in_proj(x2d, g, w_in, layer, b_f, *, seq, n_heads, tm):
    t, d = x2d.shape
    d_fox = n_heads * FOX_HEAD_DIM
    d_main = D_POOL + 3 * d_fox
    place, ones = _aug_constants(n_heads)
    kern = functools.partial(_in_proj_kernel, tiles_per_seq=seq // tm, n_heads=n_heads)
    row = lambda i: (i, 0)
    w_spec = pl.BlockSpec((None,) + w_in.shape[1:], lambda i: (layer, 0, 0),
                          pipeline_mode=pl.Buffered(1))
    return pl.pallas_call(
        kern,
        grid=(t // tm,),
        in_specs=[pl.BlockSpec((tm, d), row), _const_spec(g.shape), w_spec,
                  _const_spec(b_f.shape), _const_spec(place.shape), _const_spec(ones.shape)],
        out_specs=[pl.BlockSpec((tm, D_POOL), row), pl.BlockSpec((tm, n_heads * LANES), row),
                   pl.BlockSpec((tm, n_heads * LANES), row), pl.BlockSpec((tm, d_fox), row)],
        out_shape=[jax.ShapeDtypeStruct((t, D_POOL), F32),
                   jax.ShapeDtypeStruct((t, n_heads * LANES), BF16),
                   jax.ShapeDtypeStruct((t, n_heads * LANES), BF16),
                   jax.ShapeDtypeStruct((t, d_fox), BF16)],
        scratch_shapes=[pltpu.VMEM((1, LANES), F32), pltpu.VMEM((d, d_main), BF16),
                        pltpu.VMEM((d, LANES), BF16)],
        compiler_params=_params(1),
        name="in_proj",
    )(x2d, g, w_in, b_f, place, ones)


def _attn_kernel(q_ref, k_ref, v_ref, o_ref, m_ref, acc_ref, vtop_ref, vbot_ref, s_ref, *, tk):
    i = pl.program_id(2)
    tq = q_ref.shape[0]
    n_sub = tk // LANES

    @pl.when(i == 0)
    def _():
        v = v_ref[...].astype(F32)
        lane = lax.broadcasted_iota(jnp.int32, v.shape, 1)
        vtop_ref[:, :LANES] = jnp.where(lane < FOX_HEAD_DIM, v, 0.0).astype(BF16)
        vtop_ref[:, LANES:] = jnp.where(lane == 0, 1.0, 0.0).astype(BF16)
        vbot_ref[:, :LANES] = jnp.where(lane >= FOX_HEAD_DIM, v, 0.0).astype(BF16)
        vbot_ref[:, LANES:] = jnp.where(lane == 1, 1.0, 0.0).astype(BF16)

    m_ref[...] = jnp.full_like(m_ref, -jnp.inf)
    acc_ref[...] = jnp.zeros_like(acc_ref)
    lane2 = lax.broadcasted_iota(jnp.int32, (tq, 2 * LANES), 1)
    even_lanes = (lane2 < FOX_HEAD_DIM) | (lane2 == LANES)

    def logits(j, slot):
        ks = pl.multiple_of(j * tk, tk)
        for hh in range(2):
            q = q_ref[:, hh * LANES:(hh + 1) * LANES]
            k = k_ref[pl.ds(ks, tk), hh * LANES:(hh + 1) * LANES]
            s_ref[slot, hh] = lax.dot_general(q, k, (((1,), (1,)), ((), ())),
                                              preferred_element_type=F32)

    def update(j, slot, masked):
        ks = pl.multiple_of(j * tk, tk)
        ps, alphas = [], []
        for hh in range(2):
            s = s_ref[slot, hh]
            if masked:
                r = lax.broadcasted_iota(jnp.int32, (tq, tk), 0)
                cc = lax.broadcasted_iota(jnp.int32, (tq, tk), 1)
                s = jnp.where(r >= cc, s, -jnp.inf)
            m_prev = m_ref[hh]
            m_next = jnp.maximum(m_prev, jnp.max(s, axis=1, keepdims=True))
            alphas.append(jnp.exp2(m_prev - m_next))
            ps.append(jnp.concatenate(
                [jnp.exp2(s[:, c * LANES:(c + 1) * LANES] - m_next) for c in range(n_sub)],
                axis=1).astype(BF16))
            m_ref[hh] = m_next
        pv = jnp.dot(ps[0], vtop_ref[pl.ds(ks, tk), :], preferred_element_type=F32)
        pv = pv + jnp.dot(ps[1], vbot_ref[pl.ds(ks, tk), :], preferred_element_type=F32)
        alpha = jnp.where(even_lanes, jnp.concatenate([alphas[0]] * 2, axis=1),
                          jnp.concatenate([alphas[1]] * 2, axis=1))
        acc_ref[...] = alpha * acc_ref[...] + pv

    logits(i, 0)

    @pl.when(i == 0)
    def _():
        update(i, 0, True)

    @pl.when(i > 0)
    def _():
        logits(i - 1, 1)
        update(i, 0, True)

        def step(c, slot):
            logits(c - 1, 1 - slot)
            update(c, slot, False)

        def two_steps(tt, carry):
            c = i - 1 - 2 * tt
            step(c, 1)
            step(c - 1, 0)
            return carry

        lax.fori_loop(0, lax.shift_right_logical(i - 1, 1), two_steps, 0)

        @pl.when(i % 2 == 0)
        def _():
            step(1, 1)
            update(0, 0, False)

        @pl.when(i % 2 == 1)
        def _():
            update(0, 1, False)

    acc = acc_ref[...]
    l_e = jnp.broadcast_to(acc[:, LANES:LANES + 1], (tq, LANES))
    l_o = jnp.broadcast_to(acc[:, LANES + 1:LANES + 2], (tq, LANES))
    lane = lax.broadcasted_iota(jnp.int32, (tq, LANES), 1)
    o_ref[...] = (acc[:, :LANES] / jnp.where(lane < FOX_HEAD_DIM, l_e, l_o)).astype(o_ref.dtype)


def _attention(qa, ka, v, *, batch, seq, n_heads, tq):
    t = qa.shape[0]
    n_pairs = n_heads // 2
    nq = seq // tq
    kern = functools.partial(_attn_kernel, tk=tq)
    return pl.pallas_call(
        kern,
        grid=(batch, n_pairs, nq),
        in_specs=[pl.BlockSpec((tq, 2 * LANES), lambda b, p, i: (b * nq + i, p)),
                  pl.BlockSpec((seq, 2 * LANES), lambda b, p, i: (b, p)),
                  pl.BlockSpec((seq, LANES), lambda b, p, i: (b, p))],
        out_specs=pl.BlockSpec((tq, LANES), lambda b, p, i: (b * nq + i, p)),
        out_shape=jax.ShapeDtypeStruct((t, n_pairs * LANES), BF16),
        scratch_shapes=[pltpu.VMEM((2, tq, LANES), F32), pltpu.VMEM((tq, 2 * LANES), F32),
                        pltpu.VMEM((seq, 2 * LANES), BF16), pltpu.VMEM((seq, 2 * LANES), BF16),
                        pltpu.VMEM((2, 2, tq, tq), F32)],
        compiler_params=_params(3),
        name="fox_attn",
    )(qa, ka, v)


def _mem_kv_kernel(mem_ref, g_ref, w_ref, kv_ref):
    h = _rms(mem_ref[...], g_ref[...]).astype(BF16)
    kv_ref[...] = jnp.dot(h, w_ref[...], preferred_element_type=F32).astype(BF16)


def _mem_kv(mem2d, g, w_xkv, *, batch):
    tmem = mem2d.shape[0] // batch
    d = mem2d.shape[1]
    n = w_xkv.shape[1]
    return pl.pallas_call(
        _mem_kv_kernel,
        grid=(batch,),
        in_specs=[pl.BlockSpec((tmem, d), lambda b: (b, 0)), _const_spec(g.shape),
                  _const_spec(w_xkv.shape)],
        out_specs=pl.BlockSpec((tmem, n), lambda b: (b, 0)),
        out_shape=jax.ShapeDtypeStruct((mem2d.shape[0], n), BF16),
        compiler_params=_params(1),
        name="mem_kv",
    )(mem2d, g, w_xkv)


def _mix_kernel(x_ref, u_ref, up_ref, yf_ref, kv_ref, wp_ref, ps_ref, wmp_ref, wmf_ref,
                g_mpost_ref, g_xpre_ref, wq_ref, wo_ref, g_xpost_ref, o_ref, *, tiles_per_seq):
    i = pl.program_id(0)
    tm, d = x_ref.shape
    seq_tile = i % tiles_per_seq

    halo = jnp.where(seq_tile == 0, 0.0, up_ref[...])
    ub = jnp.concatenate([halo, u_ref[...]], axis=0)
    s2 = ub + pltpu.roll(ub, 1, 0)
    s4 = s2 + pltpu.roll(s2, 2, 0)
    s8 = s4 + pltpu.roll(s4, 4, 0)
    s16 = s8 + pltpu.roll(s8, 8, 0)
    sums = dict(zip(POOL_WINDOWS, (s2, s4, s8, s16)))
    pos = seq_tile * tm + lax.broadcasted_iota(jnp.int32, (tm, D_POOL), 0) + 1
    lane = lax.broadcasted_iota(jnp.int32, (tm, D_POOL), 1)
    pooled = jnp.zeros((tm, D_POOL), F32)
    for g, w in enumerate(POOL_WINDOWS):
        cnt = jnp.minimum(pos, w).astype(F32)
        mean = sums[w][POOL_HALO:, :] / cnt
        in_group = (lane >= g * POOL_GROUP_DIM) & (lane < (g + 1) * POOL_GROUP_DIM)
        pooled = jnp.where(in_group, mean, pooled)
    diff = (pooled - u_ref[...]).astype(BF16)
    y_pool = jnp.dot(diff, wp_ref[...], preferred_element_type=F32) * ps_ref[...]

    y = jnp.dot(y_pool.astype(BF16), wmp_ref[...], preferred_element_type=F32)
    y = y + jnp.dot(yf_ref[...], wmf_ref[...], preferred_element_type=F32)
    x1 = x_ref[...] + _rms(y, g_mpost_ref[...])

    h = _rms(x1, g_xpre_ref[...]).astype(BF16)
    q = jnp.dot(h, wq_ref[...], preferred_element_type=F32)
    dh = d // XA_HEADS
    outs = []
    for hd in range(XA_HEADS):
        qh = (q[:, hd * dh:(hd + 1) * dh] * (dh ** -0.5)).astype(BF16)
        kh = kv_ref[:, hd * dh:(hd + 1) * dh]
        vh = kv_ref[:, d + hd * dh:d + (hd + 1) * dh]
        s = lax.dot_general(qh, kh, (((1,), (1,)), ((), ())), preferred_element_type=F32)
        p = jnp.exp(s - jnp.max(s, axis=1, keepdims=True))
        o = jnp.dot(p.astype(BF16), vh, preferred_element_type=F32)
        outs.append((o / jnp.sum(p, axis=1, keepdims=True)).astype(BF16))
    att = jnp.concatenate(outs, axis=1)
    y2 = jnp.dot(att, wo_ref[...], preferred_element_type=F32)
    o_ref[...] = x1 + _rms(y2, g_xpost_ref[...])


def _mix(x2d, u, yf, kv, w_pool_bd, pool_scale, w_mix_pool, w_mix_fox, g_mpost, g_xpre,
         w_xq, w_xo, g_xpost, *, seq, tm):
    t, d = x2d.shape
    tps = seq // tm
    hb = tm // POOL_HALO
    tmem = kv.shape[0] // (t // seq)
    kern = functools.partial(_mix_kernel, tiles_per_seq=tps)
    row = lambda i: (i, 0)
    consts = (w_pool_bd, pool_scale, w_mix_pool, w_mix_fox, g_mpost, g_xpre, w_xq, w_xo, g_xpost)
    return pl.pallas_call(
        kern,
        grid=(t // tm,),
        in_specs=[pl.BlockSpec((tm, d), row), pl.BlockSpec((tm, D_POOL), row),
                  pl.BlockSpec((POOL_HALO, D_POOL), lambda i: (jnp.maximum(i * hb - 1, 0), 0)),
                  pl.BlockSpec((tm, yf.shape[1]), row),
                  pl.BlockSpec((tmem, kv.shape[1]), lambda i: (i // tps, 0))]
                 + [_const_spec(c.shape) for c in consts],
        out_specs=pl.BlockSpec((tm, d), row),
        out_shape=jax.ShapeDtypeStruct((t, d), F32),
        compiler_params=_params(1),
        name="mix_xattn",
    )(x2d, u, u, yf, kv, *consts)


def _gated_gelu(gate, up):
    a = -2.0 * LOG2E * math.sqrt(2.0 / math.pi)
    e = jnp.exp2(gate * (a + (a * 0.044715) * (gate * gate)))
    return (gate * up) / (1.0 + e)


def _ffn_kernel(x_ref, g_pre_ref, wu_ref, cw_ref, cb_ref, wd_ref, g_post_ref, o_ref, carry_ref,
                *, tiles_per_seq, ck):
    i = pl.program_id(0)
    tm = x_ref.shape[0]
    d_ff = wd_ref.shape[0]

    @pl.when(i % tiles_per_seq == 0)
    def _():
        carry_ref[...] = jnp.zeros_like(carry_ref)

    h = _rms(x_ref[...], g_pre_ref[...]).astype(BF16)

    def taps(xs, lo):
        out = xs * cw_ref[CONV_WIDTH - 1:CONV_WIDTH, lo:lo + ck]
        for tap in range(1, CONV_WIDTH):
            out = out + pltpu.roll(xs, tap, 0) * cw_ref[CONV_WIDTH - 1 - tap:CONV_WIDTH - tap, lo:lo + ck]
        return out

    def conv(c, which):
        lo = which * d_ff + c * ck
        hid = jnp.dot(h, wu_ref[:, lo:lo + ck], preferred_element_type=F32)
        prev = carry_ref[c, which]
        carry_ref[c, which] = hid[tm - CONV_CARRY:, :]
        head = taps(jnp.concatenate([prev, hid[:CONV_CARRY, :]], axis=0), lo)[CONV_CARRY:, :]
        body = taps(hid, lo)[CONV_CARRY:, :]
        return jnp.concatenate([head, body], axis=0) + cb_ref[:, lo:lo + ck]

    acc = jnp.zeros((tm, x_ref.shape[1]), F32)
    for c in range(d_ff // ck):
        act = _gated_gelu(conv(c, 0), conv(c, 1)).astype(BF16)
        acc = acc + jnp.dot(act, wd_ref[c * ck:(c + 1) * ck, :], preferred_element_type=F32)
    o_ref[...] = x_ref[...] + _rms(acc, g_post_ref[...])


def _ffn(x2d, g_pre, w_up, conv_w, conv_b, w_down, g_post, *, seq, tm, ck):
    t, d = x2d.shape
    d_ff = w_down.shape[0]
    kern = functools.partial(_ffn_kernel, tiles_per_seq=seq // tm, ck=ck)
    row = lambda i: (i, 0)
    consts = (g_pre, w_up, conv_w, conv_b, w_down, g_post)
    return pl.pallas_call(
        kern,
        grid=(t // tm,),
        in_specs=[pl.BlockSpec((tm, d), row)] + [_const_spec(a.shape) for a in consts],
        out_specs=pl.BlockSpec((tm, d), row),
        out_shape=jax.ShapeDtypeStruct((t, d), F32),
        scratch_shapes=[pltpu.VMEM((d_ff // ck, 2, CONV_CARRY, ck), F32)],
        compiler_params=_params(1),
        name="conv_ffn",
    )(x2d, *consts)


def _tile(n, pref):
    t = min(n, pref)
    assert n % t == 0, (n, t)
    return t


def kernel(x, mem, norm_mix_pre, norm_mix_post, w_in, b_forget, w_pool, pool_scale, w_mix_out,
           norm_mem, norm_xa_pre, norm_xa_post, w_xq, w_xkv, w_xo,
           norm_ffn_pre, norm_ffn_post, w_up, conv_w, conv_b, w_down):
    b, s, d = x.shape
    depth = w_in.shape[0]
    n_heads = b_forget.shape[1]
    d_fox = n_heads * FOX_HEAD_DIM
    d_main = D_POOL + 3 * d_fox
    assert n_heads % 2 == 0 and n_heads <= LANES
    assert w_in.shape[2] == d_main + n_heads
    tm = _tile(s, 512)
    tq = _tile(s, 512)
    tm_ffn = _tile(s, 512)
    ck = _tile(w_down.shape[1], 1024)

    x2d = x.reshape(b * s, d)
    mem2d = mem.reshape(b * mem.shape[1], d)
    for l in range(depth):
        row = lambda a: a[l].reshape(1, -1)
        b_f = jnp.pad(b_forget[l], (0, LANES - n_heads)).reshape(1, LANES)
        w_pool_bd = jax.scipy.linalg.block_diag(*w_pool[l]).astype(BF16)
        w_mix = w_mix_out[l].astype(BF16)

        u, qa, ka, v = _in_proj(x2d, row(norm_mix_pre), w_in, l, b_f,
                                seq=s, n_heads=n_heads, tm=tm)
        yf = _attention(qa, ka, v, batch=b, seq=s, n_heads=n_heads, tq=tq)
        kv = _mem_kv(mem2d, row(norm_mem), w_xkv[l].astype(BF16), batch=b)
        x2d = _mix(x2d, u, yf, kv, w_pool_bd, pool_scale[l].reshape(1, -1),
                   w_mix[:D_POOL], w_mix[D_POOL:], row(norm_mix_post), row(norm_xa_pre),
                   w_xq[l].astype(BF16), w_xo[l].astype(BF16), row(norm_xa_post), seq=s, tm=tm)
        x2d = _ffn(x2d, row(norm_ffn_pre), w_up[l].astype(BF16), conv_w[l].reshape(CONV_WIDTH, -1),
                   conv_b[l].reshape(1, -1), w_down[l].astype(BF16), row(norm_ffn_post),
                   seq=s, tm=tm_ffn, ck=ck)
    return x2d.reshape(b, s, d)
```

```python
import functools
import math

import jax
import jax.numpy as jnp
import numpy as np
from jax import lax
from jax.experimental import pallas as pl
from jax.experimental.pallas import tpu as pltpu

F32 = jnp.float32
BF16 = jnp.bfloat16

NORM_EPS = 1e-6
POOL_WINDOWS = (2, 4, 8, 16)
POOL_GROUP_DIM = 64
D_POOL = 256
FOX_HEAD_DIM = 64
XA_HEADS = 4
CONV_WIDTH = 3
LANES = 128
POOL_HALO = 16
CONV_CARRY = 8
AUG = 3
VMEM_LIMIT = 56 * 1024 * 1024
LOG2E = math.log2(math.e)
DIAG_SLOT = 2


def _rms(x, g):
    ms = jnp.mean(x * x, axis=-1, keepdims=True)
    return x * lax.rsqrt(ms + NORM_EPS) * g


def _const_spec(shape):
    nd = len(shape)
    return pl.BlockSpec(shape, lambda *_: (0,) * nd, pipeline_mode=pl.Buffered(1))


def _params(n_axes):
    return pltpu.CompilerParams(dimension_semantics=("arbitrary",) * n_axes,
                                vmem_limit_bytes=VMEM_LIMIT)


def _aug_constants(n_heads):
    place = np.zeros((AUG * LANES, 2 * LANES), np.float32)
    ones = np.zeros((8, 2 * LANES), np.float32)
    for h in range(n_heads):
        for n in range(AUG):
            place[n * LANES + h, 2 * AUG * h + n] = 1.0
            place[n * LANES + h, LANES + 2 * AUG * h + AUG + n] = -1.0
            ones[0, 2 * AUG * h + AUG + n] = 1.0
            ones[0, LANES + 2 * AUG * h + n] = 1.0
    return jnp.asarray(place, BF16), jnp.asarray(ones)


def _in_proj_kernel(x_ref, g_ref, w_ref, bf_ref, place_ref, ones_ref,
                    u_ref, qa_ref, ka_ref, v_ref, carry_ref, wm_ref, wf_ref,
                    *, tiles_per_seq, n_heads):
    i = pl.program_id(0)
    tm = x_ref.shape[0]
    d_fox = n_heads * FOX_HEAD_DIM
    d_main = D_POOL + 3 * d_fox

    @pl.when(i == 0)
    def _():
        blk = 2 * LANES
        for lo in range(0, d_main, blk):
            wm_ref[:, lo:lo + blk] = w_ref[lo:lo + blk, :].T.astype(BF16)
        wf_ref[...] = jnp.zeros_like(wf_ref)
        wf_ref[:n_heads, :] = w_ref[d_main:d_main + n_heads, :].astype(BF16)

    @pl.when(i % tiles_per_seq == 0)
    def _():
        carry_ref[...] = jnp.zeros_like(carry_ref)

    h = _rms(x_ref[...], g_ref[...]).astype(BF16)

    f = lax.dot_general(h, wf_ref[...], (((1,), (1,)), ((), ())),
                        preferred_element_type=F32) + bf_ref[...]
    c = jnp.minimum(f, 0.0) - jnp.log1p(jnp.exp(-jnp.abs(f)))
    row = lax.broadcasted_iota(jnp.int32, c.shape, 0)
    d = 1
    while d < tm:
        c = c + jnp.where(row >= d, pltpu.roll(c, d, 0), 0.0)
        d *= 2
    c = c + carry_ref[...]
    carry_ref[...] = c[tm - 1:tm, :]

    c2 = c * LOG2E
    hi = c2.astype(BF16)
    r1 = c2 - hi.astype(F32)
    mid = r1.astype(BF16)
    lo = (r1 - mid.astype(F32)).astype(BF16)
    pieces = jnp.concatenate([hi, mid, lo], axis=1)
    aug = jnp.dot(pieces, place_ref[...], preferred_element_type=F32) + ones_ref[0:1, :]
    aug_q, aug_k = aug[:, :LANES], aug[:, LANES:]

    main = jnp.dot(h, wm_ref[...], preferred_element_type=F32)
    u_ref[...] = main[:, :D_POOL]
    q = main[:, D_POOL:D_POOL + d_fox] * (FOX_HEAD_DIM ** -0.5 * LOG2E)
    k = main[:, D_POOL + d_fox:D_POOL + 2 * d_fox]
    v_ref[...] = main[:, D_POOL + 2 * d_fox:D_POOL + 3 * d_fox].astype(BF16)

    lane = lax.broadcasted_iota(jnp.int32, (tm, LANES), 1)
    for hd in range(n_heads):
        pair, odd = divmod(hd, 2)
        qk_lo = odd * FOX_HEAD_DIM
        a0 = (1 - odd) * FOX_HEAD_DIM
        shift = (a0 - 2 * AUG * hd) % LANES
        in_head = (lane >= qk_lo) & (lane < qk_lo + FOX_HEAD_DIM)
        in_aug = (lane >= a0) & (lane < a0 + 2 * AUG)
        sl = slice(pair * LANES, (pair + 1) * LANES)
        osl = slice(hd * LANES, (hd + 1) * LANES)
        rq = jnp.where(in_aug, pltpu.roll(aug_q, shift, 1), 0.0)
        rk = jnp.where(in_aug, pltpu.roll(aug_k, shift, 1), 0.0)
        qa_ref[:, osl] = jnp.where(in_head, q[:, sl], rq).astype(BF16)
        ka_ref[:, osl] = jnp.where(in_head, k[:, sl], rk).astype(BF16)


def _in_proj(x2d, g, w_in_t, layer, b_f, *, seq, n_heads, tm):
    t, d = x2d.shape
    d_fox = n_heads * FOX_HEAD_DIM
    d_main = D_POOL + 3 * d_fox
    place, ones = _aug_constants(n_heads)
    kern = functools.partial(_in_proj_kernel, tiles_per_seq=seq // tm, n_heads=n_heads)
    row = lambda i: (i, 0)
    return pl.pallas_call(
        kern,
        grid=(t // tm,),
        in_specs=[pl.BlockSpec((tm, d), row), _const_spec(g.shape), _layer_spec(w_in_t, layer),
                  _const_spec(b_f.shape), _const_spec(place.shape), _const_spec(ones.shape)],
        out_specs=[pl.BlockSpec((tm, D_POOL), row), pl.BlockSpec((tm, n_heads * LANES), row),
                   pl.BlockSpec((tm, n_heads * LANES), row), pl.BlockSpec((tm, d_fox), row)],
        out_shape=[jax.ShapeDtypeStruct((t, D_POOL), F32),
                   jax.ShapeDtypeStruct((t, n_heads * LANES), BF16),
                   jax.ShapeDtypeStruct((t, n_heads * LANES), BF16),
                   jax.ShapeDtypeStruct((t, d_fox), BF16)],
        scratch_shapes=[pltpu.VMEM((1, LANES), F32), pltpu.VMEM((d, d_main), BF16),
                        pltpu.VMEM((LANES, d), BF16)],
        compiler_params=_params(1),
        name="in_proj",
    )(x2d, g, w_in_t, b_f, place, ones)


def _attn_kernel(q_ref, qn_ref, k_ref, v_ref, o_ref, m_ref, acc_ref, vtop_ref, vbot_ref, s_ref,
                 *, tk, nq):
    i = pl.program_id(2)
    tq = q_ref.shape[0]
    n_sub = tk // LANES

    @pl.when(i == 0)
    def _():
        v = v_ref[...].astype(F32)
        lane = lax.broadcasted_iota(jnp.int32, v.shape, 1)
        vtop_ref[:, :LANES] = jnp.where(lane < FOX_HEAD_DIM, v, 0.0).astype(BF16)
        vtop_ref[:, LANES:] = jnp.where(lane == 0, 1.0, 0.0).astype(BF16)
        vbot_ref[:, :LANES] = jnp.where(lane >= FOX_HEAD_DIM, v, 0.0).astype(BF16)
        vbot_ref[:, LANES:] = jnp.where(lane == 1, 1.0, 0.0).astype(BF16)

    m_ref[...] = jnp.full_like(m_ref, -jnp.inf)
    acc_ref[...] = jnp.zeros_like(acc_ref)
    lane2 = lax.broadcasted_iota(jnp.int32, (tq, 2 * LANES), 1)
    even_lanes = (lane2 < FOX_HEAD_DIM) | (lane2 == LANES)

    def logits(j, slot, qr=q_ref):
        ks = pl.multiple_of(j * tk, tk)
        for hh in range(2):
            q = qr[:, hh * LANES:(hh + 1) * LANES]
            k = k_ref[pl.ds(ks, tk), hh * LANES:(hh + 1) * LANES]
            s_ref[slot, hh] = lax.dot_general(q, k, (((1,), (1,)), ((), ())),
                                              preferred_element_type=F32)

    def update(j, slot, masked):
        ks = pl.multiple_of(j * tk, tk)
        ps, alphas = [], []
        for hh in range(2):
            s = s_ref[slot, hh]
            if masked:
                r = lax.broadcasted_iota(jnp.int32, (tq, tk), 0)
                cc = lax.broadcasted_iota(jnp.int32, (tq, tk), 1)
                s = jnp.where(r >= cc, s, -jnp.inf)
            m_prev = m_ref[hh]
            m_next = jnp.maximum(m_prev, jnp.max(s, axis=1, keepdims=True))
            alphas.append(jnp.exp2(m_prev - m_next))
            ps.append(jnp.concatenate(
                [jnp.exp2(s[:, c * LANES:(c + 1) * LANES] - m_next) for c in range(n_sub)],
                axis=1).astype(BF16))
            m_ref[hh] = m_next
        pv = jnp.dot(ps[0], vtop_ref[pl.ds(ks, tk), :], preferred_element_type=F32)
        pv = pv + jnp.dot(ps[1], vbot_ref[pl.ds(ks, tk), :], preferred_element_type=F32)
        alpha = jnp.where(even_lanes, jnp.concatenate([alphas[0]] * 2, axis=1),
                          jnp.concatenate([alphas[1]] * 2, axis=1))
        acc_ref[...] = alpha * acc_ref[...] + pv

    def next_diagonal():
        logits(jnp.minimum(i + 1, nq - 1), DIAG_SLOT, qn_ref)

    @pl.when(i == 0)
    def _():
        logits(0, DIAG_SLOT)
        update(0, DIAG_SLOT, True)
        next_diagonal()

    @pl.when(i > 0)
    def _():
        logits(i - 1, 1)
        update(i, DIAG_SLOT, True)

        def step(c, slot):
            logits(c - 1, 1 - slot)
            update(c, slot, False)

        def two_steps(tt, carry):
            c = i - 1 - 2 * tt
            step(c, 1)
            step(c - 1, 0)
            return carry

        lax.fori_loop(0, lax.shift_right_logical(i - 1, 1), two_steps, 0)

        @pl.when(i % 2 == 0)
        def _():
            step(1, 1)
            next_diagonal()
            update(0, 0, False)

        @pl.when(i % 2 == 1)
        def _():
            next_diagonal()
            update(0, 1, False)

    acc = acc_ref[...]
    l_e = jnp.broadcast_to(acc[:, LANES:LANES + 1], (tq, LANES))
    l_o = jnp.broadcast_to(acc[:, LANES + 1:LANES + 2], (tq, LANES))
    lane = lax.broadcasted_iota(jnp.int32, (tq, LANES), 1)
    o_ref[...] = (acc[:, :LANES] / jnp.where(lane < FOX_HEAD_DIM, l_e, l_o)).astype(o_ref.dtype)


def _attention(qa, ka, v, *, batch, seq, n_heads, tq):
    t = qa.shape[0]
    n_pairs = n_heads // 2
    nq = seq // tq
    kern = functools.partial(_attn_kernel, tk=tq, nq=nq)
    return pl.pallas_call(
        kern,
        grid=(batch, n_pairs, nq),
        in_specs=[pl.BlockSpec((tq, 2 * LANES), lambda b, p, i: (b * nq + i, p)),
                  pl.BlockSpec((tq, 2 * LANES),
                               lambda b, p, i: (b * nq + jnp.minimum(i + 1, nq - 1), p)),
                  pl.BlockSpec((seq, 2 * LANES), lambda b, p, i: (b, p)),
                  pl.BlockSpec((seq, LANES), lambda b, p, i: (b, p))],
        out_specs=pl.BlockSpec((tq, LANES), lambda b, p, i: (b * nq + i, p)),
        out_shape=jax.ShapeDtypeStruct((t, n_pairs * LANES), BF16),
        scratch_shapes=[pltpu.VMEM((2, tq, LANES), F32), pltpu.VMEM((tq, 2 * LANES), F32),
                        pltpu.VMEM((seq, 2 * LANES), BF16), pltpu.VMEM((seq, 2 * LANES), BF16),
                        pltpu.VMEM((DIAG_SLOT + 1, 2, tq, tq), F32)],
        compiler_params=_params(3),
        name="fox_attn",
    )(qa, qa, ka, v)


def _mem_kv_kernel(mem_ref, g_ref, w_ref, kv_ref):
    h = _rms(mem_ref[...], g_ref[...]).astype(BF16)
    kv_ref[...] = jnp.dot(h, w_ref[...], preferred_element_type=F32).astype(BF16)


def _mem_kv(mem2d, g, w_xkv, *, batch):
    tmem = mem2d.shape[0] // batch
    d = mem2d.shape[1]
    n = w_xkv.shape[1]
    return pl.pallas_call(
        _mem_kv_kernel,
        grid=(batch,),
        in_specs=[pl.BlockSpec((tmem, d), lambda b: (b, 0)), _const_spec(g.shape),
                  _const_spec(w_xkv.shape)],
        out_specs=pl.BlockSpec((tmem, n), lambda b: (b, 0)),
        out_shape=jax.ShapeDtypeStruct((mem2d.shape[0], n), BF16),
        compiler_params=_params(1),
        name="mem_kv",
    )(mem2d, g, w_xkv)


def _mix_kernel(x_ref, u_ref, up_ref, yf_ref, kv_ref, wp_ref, ps_ref, wm_ref,
                g_mpost_ref, g_xpre_ref, wq_ref, wo_ref, g_xpost_ref, o_ref, wbf_ref,
                *, tiles_per_seq):
    i = pl.program_id(0)
    tm, d = x_ref.shape
    seq_tile = i % tiles_per_seq

    @pl.when(i == 0)
    def _():
        for n, w_ref in enumerate((wm_ref, wq_ref, wo_ref)):
            wbf_ref[n] = w_ref[...].astype(BF16)

    halo = jnp.where(seq_tile == 0, 0.0, up_ref[...])
    ub = jnp.concatenate([halo, u_ref[...]], axis=0)
    s2 = ub + pltpu.roll(ub, 1, 0)
    s4 = s2 + pltpu.roll(s2, 2, 0)
    s8 = s4 + pltpu.roll(s4, 4, 0)
    s16 = s8 + pltpu.roll(s8, 8, 0)
    sums = dict(zip(POOL_WINDOWS, (s2, s4, s8, s16)))
    pos = seq_tile * tm + lax.broadcasted_iota(jnp.int32, (tm, D_POOL), 0) + 1
    lane = lax.broadcasted_iota(jnp.int32, (tm, D_POOL), 1)
    pooled = jnp.zeros((tm, D_POOL), F32)
    for g, w in enumerate(POOL_WINDOWS):
        cnt = jnp.minimum(pos, w).astype(F32)
        mean = sums[w][POOL_HALO:, :] / cnt
        in_group = (lane >= g * POOL_GROUP_DIM) & (lane < (g + 1) * POOL_GROUP_DIM)
        pooled = jnp.where(in_group, mean, pooled)
    diff = (pooled - u_ref[...]).astype(BF16)
    y_pool = jnp.dot(diff, wp_ref[...], preferred_element_type=F32) * ps_ref[...]

    y = jnp.dot(y_pool.astype(BF16), wbf_ref[0, :D_POOL, :], preferred_element_type=F32)
    y = y + jnp.dot(yf_ref[...], wbf_ref[0, D_POOL:, :], preferred_element_type=F32)
    x1 = x_ref[...] + _rms(y, g_mpost_ref[...])

    h = _rms(x1, g_xpre_ref[...]).astype(BF16)
    q = jnp.dot(h, wbf_ref[1], preferred_element_type=F32)
    dh = d // XA_HEADS
    outs = []
    for hd in range(XA_HEADS):
        qh = (q[:, hd * dh:(hd + 1) * dh] * (dh ** -0.5)).astype(BF16)
        kh = kv_ref[:, hd * dh:(hd + 1) * dh]
        vh = kv_ref[:, d + hd * dh:d + (hd + 1) * dh]
        s = lax.dot_general(qh, kh, (((1,), (1,)), ((), ())), preferred_element_type=F32)
        p = jnp.exp(s - jnp.max(s, axis=1, keepdims=True))
        o = jnp.dot(p.astype(BF16), vh, preferred_element_type=F32)
        outs.append((o / jnp.sum(p, axis=1, keepdims=True)).astype(BF16))
    att = jnp.concatenate(outs, axis=1)
    y2 = jnp.dot(att, wbf_ref[2], preferred_element_type=F32)
    o_ref[...] = x1 + _rms(y2, g_xpost_ref[...])


def _layer_spec(w, layer):
    return pl.BlockSpec((None,) + w.shape[1:], lambda *_: (layer,) + (0,) * (w.ndim - 1),
                        pipeline_mode=pl.Buffered(1))


def _mix(x2d, u, yf, kv, w_pool_bd, pool_scale, w_mix_out, g_mpost, g_xpre,
         w_xq, w_xo, g_xpost, layer, *, seq, tm):
    t, d = x2d.shape
    tps = seq // tm
    hb = tm // POOL_HALO
    tmem = kv.shape[0] // (t // seq)
    kern = functools.partial(_mix_kernel, tiles_per_seq=tps)
    row = lambda i: (i, 0)
    cs = _const_spec
    return pl.pallas_call(
        kern,
        grid=(t // tm,),
        in_specs=[pl.BlockSpec((tm, d), row), pl.BlockSpec((tm, D_POOL), row),
                  pl.BlockSpec((POOL_HALO, D_POOL), lambda i: (jnp.maximum(i * hb - 1, 0), 0)),
                  pl.BlockSpec((tm, yf.shape[1]), row),
                  pl.BlockSpec((tmem, kv.shape[1]), lambda i: (i // tps, 0)),
                  cs(w_pool_bd.shape), cs(pool_scale.shape), _layer_spec(w_mix_out, layer),
                  cs(g_mpost.shape), cs(g_xpre.shape), _layer_spec(w_xq, layer),
                  _layer_spec(w_xo, layer), cs(g_xpost.shape)],
        out_specs=pl.BlockSpec((tm, d), row),
        out_shape=jax.ShapeDtypeStruct((t, d), F32),
        scratch_shapes=[pltpu.VMEM((3, d, d), BF16)],
        compiler_params=_params(1),
        name="mix_xattn",
    )(x2d, u, u, yf, kv, w_pool_bd, pool_scale, w_mix_out, g_mpost, g_xpre, w_xq, w_xo, g_xpost)


def _gated_gelu(gate, up):
    a = -2.0 * LOG2E * math.sqrt(2.0 / math.pi)
    e = jnp.exp2(gate * (a + (a * 0.044715) * (gate * gate)))
    return (gate * up) / (1.0 + e)


def _ffn_kernel(x_ref, g_pre_ref, wu_ref, cw_ref, cb_ref, wd_ref, g_post_ref, o_ref, carry_ref,
                *, tiles_per_seq, ck):
    i = pl.program_id(0)
    tm = x_ref.shape[0]
    d_ff = wd_ref.shape[0]

    @pl.when(i % tiles_per_seq == 0)
    def _():
        carry_ref[...] = jnp.zeros_like(carry_ref)

    h = _rms(x_ref[...], g_pre_ref[...]).astype(BF16)

    def taps(xs, lo):
        out = xs * cw_ref[CONV_WIDTH - 1:CONV_WIDTH, lo:lo + ck]
        for tap in range(1, CONV_WIDTH):
            out = out + pltpu.roll(xs, tap, 0) * cw_ref[CONV_WIDTH - 1 - tap:CONV_WIDTH - tap, lo:lo + ck]
        return out

    def conv(c, which):
        lo = which * d_ff + c * ck
        hid = jnp.dot(h, wu_ref[:, lo:lo + ck], preferred_element_type=F32)
        prev = carry_ref[c, which]
        carry_ref[c, which] = hid[tm - CONV_CARRY:, :]
        head = taps(jnp.concatenate([prev, hid[:CONV_CARRY, :]], axis=0), lo)[CONV_CARRY:, :]
        body = taps(hid, lo)[CONV_CARRY:, :]
        return jnp.concatenate([head, body], axis=0) + cb_ref[:, lo:lo + ck]

    acc = jnp.zeros((tm, x_ref.shape[1]), F32)
    for c in range(d_ff // ck):
        act = _gated_gelu(conv(c, 0), conv(c, 1)).astype(BF16)
        acc = acc + jnp.dot(act, wd_ref[c * ck:(c + 1) * ck, :], preferred_element_type=F32)
    o_ref[...] = x_ref[...] + _rms(acc, g_post_ref[...])


def _ffn(x2d, g_pre, w_up, conv_w, conv_b, w_down, g_post, *, seq, tm, ck):
    t, d = x2d.shape
    d_ff = w_down.shape[0]
    kern = functools.partial(_ffn_kernel, tiles_per_seq=seq // tm, ck=ck)
    row = lambda i: (i, 0)
    consts = (g_pre, w_up, conv_w, conv_b, w_down, g_post)
    return pl.pallas_call(
        kern,
        grid=(t // tm,),
        in_specs=[pl.BlockSpec((tm, d), row)] + [_const_spec(a.shape) for a in consts],
        out_specs=pl.BlockSpec((tm, d), row),
        out_shape=jax.ShapeDtypeStruct((t, d), F32),
        scratch_shapes=[pltpu.VMEM((d_ff // ck, 2, CONV_CARRY, ck), F32)],
        compiler_params=_params(1),
        name="conv_ffn",
    )(x2d, *consts)


def _tile(n, pref):
    t = min(n, pref)
    assert n % t == 0, (n, t)
    return t


def kernel(x, mem, norm_mix_pre, norm_mix_post, w_in, b_forget, w_pool, pool_scale, w_mix_out,
           norm_mem, norm_xa_pre, norm_xa_post, w_xq, w_xkv, w_xo,
           norm_ffn_pre, norm_ffn_post, w_up, conv_w, conv_b, w_down):
    b, s, d = x.shape
    depth = w_in.shape[0]
    n_heads = b_forget.shape[1]
    d_fox = n_heads * FOX_HEAD_DIM
    d_main = D_POOL + 3 * d_fox
    assert n_heads % 2 == 0 and n_heads <= LANES
    assert w_in.shape[2] == d_main + n_heads
    tm = _tile(s, 512)
    tq = _tile(s, 512)
    tm_ffn = _tile(s, 512)
    ck = _tile(w_down.shape[1], 1024)

    x2d = x.reshape(b * s, d)
    mem2d = mem.reshape(b * mem.shape[1], d)
    w_in_t = jnp.swapaxes(w_in, 1, 2)
    for l in range(depth):
        row = lambda a: a[l].reshape(1, -1)
        b_f = jnp.pad(b_forget[l], (0, LANES - n_heads)).reshape(1, LANES)
        w_pool_bd = jax.scipy.linalg.block_diag(*w_pool[l]).astype(BF16)

        u, qa, ka, v = _in_proj(x2d, row(norm_mix_pre), w_in_t, l, b_f,
                                seq=s, n_heads=n_heads, tm=tm)
        yf = _attention(qa, ka, v, batch=b, seq=s, n_heads=n_heads, tq=tq)
        kv = _mem_kv(mem2d, row(norm_mem), w_xkv[l].astype(BF16), batch=b)
        x2d = _mix(x2d, u, yf, kv, w_pool_bd, pool_scale[l].reshape(1, -1), w_mix_out,
                   row(norm_mix_post), row(norm_xa_pre), w_xq, w_xo, row(norm_xa_post), l,
                   seq=s, tm=tm)
        x2d = _ffn(x2d, row(norm_ffn_pre), w_up[l].astype(BF16), conv_w[l].reshape(CONV_WIDTH, -1),
                   conv_b[l].reshape(1, -1), w_down[l].astype(BF16), row(norm_ffn_post),
                   seq=s, tm=tm_ffn, ck=ck)
    return x2d.reshape(b, s, d)
```

```python
import functools
import math

import jax
import jax.numpy as jnp
import numpy as np
from jax import lax
from jax.experimental import pallas as pl
from jax.experimental.pallas import tpu as pltpu

F32 = jnp.float32
BF16 = jnp.bfloat16

NORM_EPS = 1e-6
POOL_WINDOWS = (2, 4, 8, 16)
POOL_GROUP_DIM = 64
D_POOL = 256
FOX_HEAD_DIM = 64
XA_HEADS = 4
CONV_WIDTH = 3
LANES = 128
POOL_HALO = 16
CONV_CARRY = 8
AUG = 3
VMEM_LIMIT = 56 * 1024 * 1024
LOG2E = math.log2(math.e)
DIAG_SLOT = 2


def _rms(x, g):
    ms = jnp.mean(x * x, axis=-1, keepdims=True)
    return x * lax.rsqrt(ms + NORM_EPS) * g


def _const_spec(shape):
    nd = len(shape)
    return pl.BlockSpec(shape, lambda *_: (0,) * nd, pipeline_mode=pl.Buffered(1))


def _params(n_axes):
    return pltpu.CompilerParams(dimension_semantics=("arbitrary",) * n_axes,
                                vmem_limit_bytes=VMEM_LIMIT)


def _aug_constants(n_heads):
    place = np.zeros((AUG * LANES, 2 * LANES), np.float32)
    ones = np.zeros((8, 2 * LANES), np.float32)
    for h in range(n_heads):
        for n in range(AUG):
            place[n * LANES + h, 2 * AUG * h + n] = 1.0
            place[n * LANES + h, LANES + 2 * AUG * h + AUG + n] = -1.0
            ones[0, 2 * AUG * h + AUG + n] = 1.0
            ones[0, LANES + 2 * AUG * h + n] = 1.0
    return jnp.asarray(place, BF16), jnp.asarray(ones)


def _in_proj_kernel(x_ref, g_ref, w_ref, bf_ref, place_ref, ones_ref,
                    u_ref, qa_ref, ka_ref, v_ref, carry_ref, wm_ref, wf_ref,
                    *, tiles_per_seq, n_heads):
    i = pl.program_id(0)
    tm = x_ref.shape[0]
    d_fox = n_heads * FOX_HEAD_DIM
    d_main = D_POOL + 3 * d_fox

    @pl.when(i == 0)
    def _():
        blk = 2 * LANES
        for lo in range(0, d_main, blk):
            wm_ref[:, lo:lo + blk] = w_ref[lo:lo + blk, :].T.astype(BF16)
        wf_ref[...] = jnp.zeros_like(wf_ref)
        wf_ref[:n_heads, :] = w_ref[d_main:d_main + n_heads, :].astype(BF16)

    @pl.when(i % tiles_per_seq == 0)
    def _():
        carry_ref[...] = jnp.zeros_like(carry_ref)

    h = _rms(x_ref[...], g_ref[...]).astype(BF16)

    f = lax.dot_general(h, wf_ref[...], (((1,), (1,)), ((), ())),
                        preferred_element_type=F32) + bf_ref[...]
    c = jnp.minimum(f, 0.0) - jnp.log1p(jnp.exp(-jnp.abs(f)))
    row = lax.broadcasted_iota(jnp.int32, c.shape, 0)
    d = 1
    while d < tm:
        c = c + jnp.where(row >= d, pltpu.roll(c, d, 0), 0.0)
        d *= 2
    c = c + carry_ref[...]
    carry_ref[...] = c[tm - 1:tm, :]

    c2 = c * LOG2E
    hi = c2.astype(BF16)
    r1 = c2 - hi.astype(F32)
    mid = r1.astype(BF16)
    lo = (r1 - mid.astype(F32)).astype(BF16)
    pieces = jnp.concatenate([hi, mid, lo], axis=1)
    aug = jnp.dot(pieces, place_ref[...], preferred_element_type=F32) + ones_ref[0:1, :]
    aug_q, aug_k = aug[:, :LANES], aug[:, LANES:]

    main = jnp.dot(h, wm_ref[...], preferred_element_type=F32)
    u_ref[...] = main[:, :D_POOL]
    q = main[:, D_POOL:D_POOL + d_fox] * (FOX_HEAD_DIM ** -0.5 * LOG2E)
    k = main[:, D_POOL + d_fox:D_POOL + 2 * d_fox]
    v_ref[...] = main[:, D_POOL + 2 * d_fox:D_POOL + 3 * d_fox].astype(BF16)

    lane = lax.broadcasted_iota(jnp.int32, (tm, LANES), 1)
    for hd in range(n_heads):
        pair, odd = divmod(hd, 2)
        qk_lo = odd * FOX_HEAD_DIM
        a0 = (1 - odd) * FOX_HEAD_DIM
        shift = (a0 - 2 * AUG * hd) % LANES
        in_head = (lane >= qk_lo) & (lane < qk_lo + FOX_HEAD_DIM)
        in_aug = (lane >= a0) & (lane < a0 + 2 * AUG)
        sl = slice(pair * LANES, (pair + 1) * LANES)
        osl = slice(hd * LANES, (hd + 1) * LANES)
        rq = jnp.where(in_aug, pltpu.roll(aug_q, shift, 1), 0.0)
        rk = jnp.where(in_aug, pltpu.roll(aug_k, shift, 1), 0.0)
        qa_ref[:, osl] = jnp.where(in_head, q[:, sl], rq).astype(BF16)
        ka_ref[:, osl] = jnp.where(in_head, k[:, sl], rk).astype(BF16)


def _in_proj(x2d, g, w_in_t, layer, b_f, *, seq, n_heads, tm):
    t, d = x2d.shape
    d_fox = n_heads * FOX_HEAD_DIM
    d_main = D_POOL + 3 * d_fox
    place, ones = _aug_constants(n_heads)
    kern = functools.partial(_in_proj_kernel, tiles_per_seq=seq // tm, n_heads=n_heads)
    row = lambda i: (i, 0)
    return pl.pallas_call(
        kern,
        grid=(t // tm,),
        in_specs=[pl.BlockSpec((tm, d), row), _const_spec(g.shape), _layer_spec(w_in_t, layer),
                  _const_spec(b_f.shape), _const_spec(place.shape), _const_spec(ones.shape)],
        out_specs=[pl.BlockSpec((tm, D_POOL), row), pl.BlockSpec((tm, n_heads * LANES), row),
                   pl.BlockSpec((tm, n_heads * LANES), row), pl.BlockSpec((tm, d_fox), row)],
        out_shape=[jax.ShapeDtypeStruct((t, D_POOL), F32),
                   jax.ShapeDtypeStruct((t, n_heads * LANES), BF16),
                   jax.ShapeDtypeStruct((t, n_heads * LANES), BF16),
                   jax.ShapeDtypeStruct((t, d_fox), BF16)],
        scratch_shapes=[pltpu.VMEM((1, LANES), F32), pltpu.VMEM((d, d_main), BF16),
                        pltpu.VMEM((LANES, d), BF16)],
        compiler_params=_params(1),
        name="in_proj",
    )(x2d, g, w_in_t, b_f, place, ones)


def _attn_kernel(q_ref, qn_ref, k_ref, v_ref, o_ref, m_ref, acc_ref, vtop_ref, vbot_ref, s_ref,
                 *, tk, nq):
    i = pl.program_id(2)
    tq = q_ref.shape[0]
    n_sub = tk // LANES

    def build_v():
        v = v_ref[...].astype(F32)
        lane = lax.broadcasted_iota(jnp.int32, v.shape, 1)
        vtop_ref[:, :LANES] = jnp.where(lane < FOX_HEAD_DIM, v, 0.0).astype(BF16)
        vtop_ref[:, LANES:] = jnp.where(lane == 0, 1.0, 0.0).astype(BF16)
        vbot_ref[:, :LANES] = jnp.where(lane >= FOX_HEAD_DIM, v, 0.0).astype(BF16)
        vbot_ref[:, LANES:] = jnp.where(lane == 1, 1.0, 0.0).astype(BF16)

    lane2 = lax.broadcasted_iota(jnp.int32, (tq, 2 * LANES), 1)
    even_lanes = (lane2 < FOX_HEAD_DIM) | (lane2 == LANES)

    def pv_of(ps, j):
        ks = pl.multiple_of(j * tk, tk)
        pv = jnp.dot(ps[0], vtop_ref[pl.ds(ks, tk), :], preferred_element_type=F32)
        return pv + jnp.dot(ps[1], vbot_ref[pl.ds(ks, tk), :], preferred_element_type=F32)

    def update_diagonal():
        tri = (lax.broadcasted_iota(jnp.int32, (LANES, LANES), 0)
               >= lax.broadcasted_iota(jnp.int32, (LANES, LANES), 1))
        ps = []
        for hh in range(2):
            s = s_ref[DIAG_SLOT, hh]
            cols = []
            for c in range(n_sub):
                lo, hi = c * LANES, (c + 1) * LANES
                parts = [jnp.full((lo, LANES), -jnp.inf, F32)] if c else []
                parts.append(jnp.where(tri, s[lo:hi, lo:hi], -jnp.inf))
                if hi < tq:
                    parts.append(s[hi:, lo:hi])
                cols.append(jnp.concatenate(parts, axis=0))
            best = functools.reduce(jnp.maximum, cols)
            m_next = jnp.broadcast_to(jnp.max(best, axis=1, keepdims=True), (tq, LANES))
            p_cols = []
            for c in range(n_sub):
                lo = c * LANES
                parts = [jnp.zeros((lo, LANES), F32)] if c else []
                parts.append(jnp.exp2(cols[c][lo:, :] - m_next[lo:, :]))
                p_cols.append(jnp.concatenate(parts, axis=0))
            ps.append(jnp.concatenate(p_cols, axis=1).astype(BF16))
            m_ref[hh] = m_next
        acc_ref[...] = pv_of(ps, i)

    def logits(j, slot, qr=q_ref):
        ks = pl.multiple_of(j * tk, tk)
        for hh in range(2):
            q = qr[:, hh * LANES:(hh + 1) * LANES]
            k = k_ref[pl.ds(ks, tk), hh * LANES:(hh + 1) * LANES]
            s_ref[slot, hh] = lax.dot_general(q, k, (((1,), (1,)), ((), ())),
                                              preferred_element_type=F32)

    def update(j, slot):
        ps, alphas = [], []
        for hh in range(2):
            s = s_ref[slot, hh]
            m_prev = m_ref[hh]
            m_next = jnp.maximum(m_prev, jnp.max(s, axis=1, keepdims=True))
            alphas.append(jnp.exp2(m_prev - m_next))
            ps.append(jnp.concatenate(
                [jnp.exp2(s[:, c * LANES:(c + 1) * LANES] - m_next) for c in range(n_sub)],
                axis=1).astype(BF16))
            m_ref[hh] = m_next
        alpha = jnp.where(even_lanes, jnp.concatenate([alphas[0]] * 2, axis=1),
                          jnp.concatenate([alphas[1]] * 2, axis=1))
        acc_ref[...] = alpha * acc_ref[...] + pv_of(ps, j)

    def finish():
        acc = acc_ref[...]
        l_e = jnp.broadcast_to(acc[:, LANES:LANES + 1], (tq, LANES))
        l_o = jnp.broadcast_to(acc[:, LANES + 1:LANES + 2], (tq, LANES))
        lane = lax.broadcasted_iota(jnp.int32, (tq, LANES), 1)
        o_ref[...] = (acc[:, :LANES] / jnp.where(lane < FOX_HEAD_DIM, l_e, l_o)).astype(o_ref.dtype)

    def next_diagonal():
        logits(jnp.minimum(i + 1, nq - 1), DIAG_SLOT, qn_ref)

    @pl.when(i == 0)
    def _():
        build_v()
        logits(0, DIAG_SLOT)
        update_diagonal()
        next_diagonal()
        finish()

    @pl.when(i > 0)
    def _():
        logits(i - 1, 1)
        update_diagonal()

        def step(c, slot):
            logits(c - 1, 1 - slot)
            update(c, slot)

        def two_steps(tt, carry):
            c = i - 1 - 2 * tt
            step(c, 1)
            step(c - 1, 0)
            return carry

        lax.fori_loop(0, lax.shift_right_logical(i - 1, 1), two_steps, 0)

        @pl.when(i % 2 == 0)
        def _():
            step(1, 1)
            next_diagonal()
            update(0, 0)
            finish()

        @pl.when(i % 2 == 1)
        def _():
            next_diagonal()
            update(0, 1)
            finish()


def _attention(qa, ka, v, *, batch, seq, n_heads, tq):
    t = qa.shape[0]
    n_pairs = n_heads // 2
    nq = seq // tq
    kern = functools.partial(_attn_kernel, tk=tq, nq=nq)
    return pl.pallas_call(
        kern,
        grid=(batch, n_pairs, nq),
        in_specs=[pl.BlockSpec((tq, 2 * LANES), lambda b, p, i: (b * nq + i, p)),
                  pl.BlockSpec((tq, 2 * LANES),
                               lambda b, p, i: (b * nq + jnp.minimum(i + 1, nq - 1), p)),
                  pl.BlockSpec((seq, 2 * LANES), lambda b, p, i: (b, p)),
                  pl.BlockSpec((seq, LANES), lambda b, p, i: (b, p))],
        out_specs=pl.BlockSpec((tq, LANES), lambda b, p, i: (b * nq + i, p)),
        out_shape=jax.ShapeDtypeStruct((t, n_pairs * LANES), BF16),
        scratch_shapes=[pltpu.VMEM((2, tq, LANES), F32), pltpu.VMEM((tq, 2 * LANES), F32),
                        pltpu.VMEM((seq, 2 * LANES), BF16), pltpu.VMEM((seq, 2 * LANES), BF16),
                        pltpu.VMEM((DIAG_SLOT + 1, 2, tq, tq), F32)],
        compiler_params=_params(3),
        name="fox_attn",
    )(qa, qa, ka, v)


def _mem_kv_kernel(mem_ref, g_ref, w_ref, kv_ref):
    h = _rms(mem_ref[...], g_ref[...]).astype(BF16)
    kv_ref[...] = jnp.dot(h, w_ref[...], preferred_element_type=F32).astype(BF16)


def _mem_kv(mem2d, g, w_xkv, *, batch):
    tmem = mem2d.shape[0] // batch
    d = mem2d.shape[1]
    n = w_xkv.shape[1]
    return pl.pallas_call(
        _mem_kv_kernel,
        grid=(batch,),
        in_specs=[pl.BlockSpec((tmem, d), lambda b: (b, 0)), _const_spec(g.shape),
                  _const_spec(w_xkv.shape)],
        out_specs=pl.BlockSpec((tmem, n), lambda b: (b, 0)),
        out_shape=jax.ShapeDtypeStruct((mem2d.shape[0], n), BF16),
        compiler_params=_params(1),
        name="mem_kv",
    )(mem2d, g, w_xkv)


def _mix_kernel(x_ref, u_ref, up_ref, yf_ref, kv_ref, wp_ref, ps_ref, wm_ref,
                g_mpost_ref, g_xpre_ref, wq_ref, wo_ref, g_xpost_ref, o_ref, wbf_ref,
                *, tiles_per_seq):
    i = pl.program_id(0)
    tm, d = x_ref.shape
    seq_tile = i % tiles_per_seq

    @pl.when(i == 0)
    def _():
        for n, w_ref in enumerate((wm_ref, wq_ref, wo_ref)):
            wbf_ref[n] = w_ref[...].astype(BF16)

    halo = jnp.where(seq_tile == 0, 0.0, up_ref[...])
    ub = jnp.concatenate([halo, u_ref[...]], axis=0)
    s2 = ub + pltpu.roll(ub, 1, 0)
    s4 = s2 + pltpu.roll(s2, 2, 0)
    s8 = s4 + pltpu.roll(s4, 4, 0)
    s16 = s8 + pltpu.roll(s8, 8, 0)
    sums = dict(zip(POOL_WINDOWS, (s2, s4, s8, s16)))
    pos = seq_tile * tm + lax.broadcasted_iota(jnp.int32, (tm, D_POOL), 0) + 1
    lane = lax.broadcasted_iota(jnp.int32, (tm, D_POOL), 1)
    pooled = jnp.zeros((tm, D_POOL), F32)
    for g, w in enumerate(POOL_WINDOWS):
        cnt = jnp.minimum(pos, w).astype(F32)
        mean = sums[w][POOL_HALO:, :] / cnt
        in_group = (lane >= g * POOL_GROUP_DIM) & (lane < (g + 1) * POOL_GROUP_DIM)
        pooled = jnp.where(in_group, mean, pooled)
    diff = (pooled - u_ref[...]).astype(BF16)
    y_pool = jnp.dot(diff, wp_ref[...], preferred_element_type=F32) * ps_ref[...]

    y = jnp.dot(y_pool.astype(BF16), wbf_ref[0, :D_POOL, :], preferred_element_type=F32)
    y = y + jnp.dot(yf_ref[...], wbf_ref[0, D_POOL:, :], preferred_element_type=F32)
    x1 = x_ref[...] + _rms(y, g_mpost_ref[...])

    h = _rms(x1, g_xpre_ref[...]).astype(BF16)
    q = jnp.dot(h, wbf_ref[1], preferred_element_type=F32)
    dh = d // XA_HEADS
    outs = []
    for hd in range(XA_HEADS):
        qh = (q[:, hd * dh:(hd + 1) * dh] * (dh ** -0.5)).astype(BF16)
        kh = kv_ref[:, hd * dh:(hd + 1) * dh]
        vh = kv_ref[:, d + hd * dh:d + (hd + 1) * dh]
        s = lax.dot_general(qh, kh, (((1,), (1,)), ((), ())), preferred_element_type=F32)
        p = jnp.exp(s - jnp.max(s, axis=1, keepdims=True))
        o = jnp.dot(p.astype(BF16), vh, preferred_element_type=F32)
        outs.append((o / jnp.sum(p, axis=1, keepdims=True)).astype(BF16))
    att = jnp.concatenate(outs, axis=1)
    y2 = jnp.dot(att, wbf_ref[2], preferred_element_type=F32)
    o_ref[...] = x1 + _rms(y2, g_xpost_ref[...])


def _layer_spec(w, layer):
    return pl.BlockSpec((None,) + w.shape[1:], lambda *_: (layer,) + (0,) * (w.ndim - 1),
                        pipeline_mode=pl.Buffered(1))


def _mix(x2d, u, yf, kv, w_pool_bd, pool_scale, w_mix_out, g_mpost, g_xpre,
         w_xq, w_xo, g_xpost, layer, *, seq, tm):
    t, d = x2d.shape
    tps = seq // tm
    hb = tm // POOL_HALO
    tmem = kv.shape[0] // (t // seq)
    kern = functools.partial(_mix_kernel, tiles_per_seq=tps)
    row = lambda i: (i, 0)
    cs = _const_spec
    return pl.pallas_call(
        kern,
        grid=(t // tm,),
        in_specs=[pl.BlockSpec((tm, d), row), pl.BlockSpec((tm, D_POOL), row),
                  pl.BlockSpec((POOL_HALO, D_POOL), lambda i: (jnp.maximum(i * hb - 1, 0), 0)),
                  pl.BlockSpec((tm, yf.shape[1]), row),
                  pl.BlockSpec((tmem, kv.shape[1]), lambda i: (i // tps, 0)),
                  cs(w_pool_bd.shape), cs(pool_scale.shape), _layer_spec(w_mix_out, layer),
                  cs(g_mpost.shape), cs(g_xpre.shape), _layer_spec(w_xq, layer),
                  _layer_spec(w_xo, layer), cs(g_xpost.shape)],
        out_specs=pl.BlockSpec((tm, d), row),
        out_shape=jax.ShapeDtypeStruct((t, d), F32),
        scratch_shapes=[pltpu.VMEM((3, d, d), BF16)],
        compiler_params=_params(1),
        name="mix_xattn",
    )(x2d, u, u, yf, kv, w_pool_bd, pool_scale, w_mix_out, g_mpost, g_xpre, w_xq, w_xo, g_xpost)


def _gated_gelu(gate, up):
    a = -2.0 * LOG2E * math.sqrt(2.0 / math.pi)
    e = jnp.exp2(gate * (a + (a * 0.044715) * (gate * gate)))
    return (gate * up) / (1.0 + e)


def _ffn_kernel(x_ref, g_pre_ref, wu_ref, cw_ref, cb_ref, wd_ref, g_post_ref, o_ref, carry_ref,
                *, tiles_per_seq, ck):
    i = pl.program_id(0)
    tm = x_ref.shape[0]
    d_ff = wd_ref.shape[0]

    @pl.when(i % tiles_per_seq == 0)
    def _():
        carry_ref[...] = jnp.zeros_like(carry_ref)

    h = _rms(x_ref[...], g_pre_ref[...]).astype(BF16)

    def taps(xs, lo):
        out = xs * cw_ref[CONV_WIDTH - 1:CONV_WIDTH, lo:lo + ck]
        for tap in range(1, CONV_WIDTH):
            out = out + pltpu.roll(xs, tap, 0) * cw_ref[CONV_WIDTH - 1 - tap:CONV_WIDTH - tap, lo:lo + ck]
        return out

    def conv(c, which):
        lo = which * d_ff + c * ck
        hid = jnp.dot(h, wu_ref[:, lo:lo + ck], preferred_element_type=F32)
        prev = carry_ref[c, which]
        carry_ref[c, which] = hid[tm - CONV_CARRY:, :]
        head = taps(jnp.concatenate([prev, hid[:CONV_CARRY, :]], axis=0), lo)[CONV_CARRY:, :]
        body = taps(hid, lo)[CONV_CARRY:, :]
        return jnp.concatenate([head, body], axis=0) + cb_ref[:, lo:lo + ck]

    acc = jnp.zeros((tm, x_ref.shape[1]), F32)
    for c in range(d_ff // ck):
        act = _gated_gelu(conv(c, 0), conv(c, 1)).astype(BF16)
        acc = acc + jnp.dot(act, wd_ref[c * ck:(c + 1) * ck, :], preferred_element_type=F32)
    o_ref[...] = x_ref[...] + _rms(acc, g_post_ref[...])


def _ffn(x2d, g_pre, w_up, conv_w, conv_b, w_down, g_post, *, seq, tm, ck):
    t, d = x2d.shape
    d_ff = w_down.shape[0]
    kern = functools.partial(_ffn_kernel, tiles_per_seq=seq // tm, ck=ck)
    row = lambda i: (i, 0)
    consts = (g_pre, w_up, conv_w, conv_b, w_down, g_post)
    return pl.pallas_call(
        kern,
        grid=(t // tm,),
        in_specs=[pl.BlockSpec((tm, d), row)] + [_const_spec(a.shape) for a in consts],
        out_specs=pl.BlockSpec((tm, d), row),
        out_shape=jax.ShapeDtypeStruct((t, d), F32),
        scratch_shapes=[pltpu.VMEM((d_ff // ck, 2, CONV_CARRY, ck), F32)],
        compiler_params=_params(1),
        name="conv_ffn",
    )(x2d, *consts)


def _tile(n, pref):
    t = min(n, pref)
    assert n % t == 0, (n, t)
    return t


def kernel(x, mem, norm_mix_pre, norm_mix_post, w_in, b_forget, w_pool, pool_scale, w_mix_out,
           norm_mem, norm_xa_pre, norm_xa_post, w_xq, w_xkv, w_xo,
           norm_ffn_pre, norm_ffn_post, w_up, conv_w, conv_b, w_down):
    b, s, d = x.shape
    depth = w_in.shape[0]
    n_heads = b_forget.shape[1]
    d_fox = n_heads * FOX_HEAD_DIM
    d_main = D_POOL + 3 * d_fox
    assert n_heads % 2 == 0 and n_heads <= LANES
    assert w_in.shape[2] == d_main + n_heads
    tm = _tile(s, 512)
    tq = _tile(s, 512)
    tm_ffn = _tile(s, 512)
    ck = _tile(w_down.shape[1], 4096)

    x2d = x.reshape(b * s, d)
    mem2d = mem.reshape(b * mem.shape[1], d)
    w_in_t = jnp.swapaxes(w_in, 1, 2)
    for l in range(depth):
        row = lambda a: a[l].reshape(1, -1)
        b_f = jnp.pad(b_forget[l], (0, LANES - n_heads)).reshape(1, LANES)
        w_pool_bd = jax.scipy.linalg.block_diag(*w_pool[l]).astype(BF16)

        u, qa, ka, v = _in_proj(x2d, row(norm_mix_pre), w_in_t, l, b_f,
                                seq=s, n_heads=n_heads, tm=tm)
        yf = _attention(qa, ka, v, batch=b, seq=s, n_heads=n_heads, tq=tq)
        kv = _mem_kv(mem2d, row(norm_mem), w_xkv[l].astype(BF16), batch=b)
        x2d = _mix(x2d, u, yf, kv, w_pool_bd, pool_scale[l].reshape(1, -1), w_mix_out,
                   row(norm_mix_post), row(norm_xa_pre), w_xq, w_xo, row(norm_xa_post), l,
                   seq=s, tm=tm)
        x2d = _ffn(x2d, row(norm_ffn_pre), w_up[l].astype(BF16), conv_w[l].reshape(CONV_WIDTH, -1),
                   conv_b[l].reshape(1, -1), w_down[l].astype(BF16), row(norm_ffn_post),
                   seq=s, tm=tm_ffn, ck=ck)
    return x2d.reshape(b, s, d)
```

```python
import functools
import math

import jax
import jax.numpy as jnp
import numpy as np
from jax import lax
from jax.experimental import pallas as pl
from jax.experimental.pallas import tpu as pltpu

F32 = jnp.float32
BF16 = jnp.bfloat16

NORM_EPS = 1e-6
POOL_WINDOWS = (2, 4, 8, 16)
POOL_GROUP_DIM = 64
D_POOL = 256
FOX_HEAD_DIM = 64
XA_HEADS = 4
CONV_WIDTH = 3
LANES = 128
POOL_HALO = 16
CONV_CARRY = 8
BF16_SUBLANES = 16
AUG = 3
VMEM_LIMIT = 56 * 1024 * 1024
LOG2E = math.log2(math.e)
DIAG_SLOT = 2


def _rms(x, g):
    ms = jnp.mean(x * x, axis=-1, keepdims=True)
    return x * lax.rsqrt(ms + NORM_EPS) * g


def _const_spec(shape):
    nd = len(shape)
    return pl.BlockSpec(shape, lambda *_: (0,) * nd, pipeline_mode=pl.Buffered(1))


def _params(n_axes):
    return pltpu.CompilerParams(dimension_semantics=("arbitrary",) * n_axes,
                                vmem_limit_bytes=VMEM_LIMIT)


def _aug_constants(n_heads):
    place = np.zeros((AUG * LANES, 2 * LANES), np.float32)
    ones = np.zeros((8, 2 * LANES), np.float32)
    for h in range(n_heads):
        for n in range(AUG):
            place[n * LANES + h, 2 * AUG * h + n] = 1.0
            place[n * LANES + h, LANES + 2 * AUG * h + AUG + n] = -1.0
            ones[0, 2 * AUG * h + AUG + n] = 1.0
            ones[0, LANES + 2 * AUG * h + n] = 1.0
    return jnp.asarray(place, BF16), jnp.asarray(ones)


def _in_proj_kernel(x_ref, g_ref, w_ref, bf_ref, place_ref, ones_ref,
                    u_ref, qa_ref, ka_ref, v_ref, carry_ref, wm_ref, wf_ref,
                    *, tiles_per_seq, n_heads):
    i = pl.program_id(0)
    tm = x_ref.shape[0]
    d_fox = n_heads * FOX_HEAD_DIM
    d_main = D_POOL + 3 * d_fox

    @pl.when(i == 0)
    def _():
        blk = 2 * LANES
        for lo in range(0, d_main, blk):
            wm_ref[:, lo:lo + blk] = w_ref[lo:lo + blk, :].T.astype(BF16)
        wf_ref[...] = jnp.zeros_like(wf_ref)
        wf_ref[:n_heads, :] = w_ref[d_main:d_main + n_heads, :].astype(BF16)

    @pl.when(i % tiles_per_seq == 0)
    def _():
        carry_ref[...] = jnp.zeros_like(carry_ref)

    h = _rms(x_ref[...], g_ref[...]).astype(BF16)

    f = lax.dot_general(h, wf_ref[...], (((1,), (1,)), ((), ())),
                        preferred_element_type=F32) + bf_ref[...]
    c = jnp.minimum(f, 0.0) - jnp.log1p(jnp.exp(-jnp.abs(f)))
    row = lax.broadcasted_iota(jnp.int32, c.shape, 0)
    d = 1
    while d < tm:
        c = c + jnp.where(row >= d, pltpu.roll(c, d, 0), 0.0)
        d *= 2
    c = c + carry_ref[...]
    carry_ref[...] = c[tm - 1:tm, :]

    c2 = c * LOG2E
    hi = c2.astype(BF16)
    r1 = c2 - hi.astype(F32)
    mid = r1.astype(BF16)
    lo = (r1 - mid.astype(F32)).astype(BF16)
    pieces = jnp.concatenate([hi, mid, lo], axis=1)
    aug = jnp.dot(pieces, place_ref[...], preferred_element_type=F32) + ones_ref[0:1, :]
    aug_q, aug_k = aug[:, :LANES], aug[:, LANES:]

    main = jnp.dot(h, wm_ref[...], preferred_element_type=F32)
    u_ref[...] = main[:, :D_POOL]
    q = main[:, D_POOL:D_POOL + d_fox] * (FOX_HEAD_DIM ** -0.5 * LOG2E)
    k = main[:, D_POOL + d_fox:D_POOL + 2 * d_fox]
    v_ref[...] = main[:, D_POOL + 2 * d_fox:D_POOL + 3 * d_fox].astype(BF16)

    lane = lax.broadcasted_iota(jnp.int32, (tm, LANES), 1)
    for hd in range(n_heads):
        pair, odd = divmod(hd, 2)
        qk_lo = odd * FOX_HEAD_DIM
        a0 = (1 - odd) * FOX_HEAD_DIM
        shift = (a0 - 2 * AUG * hd) % LANES
        in_head = (lane >= qk_lo) & (lane < qk_lo + FOX_HEAD_DIM)
        in_aug = (lane >= a0) & (lane < a0 + 2 * AUG)
        sl = slice(pair * LANES, (pair + 1) * LANES)
        osl = slice(hd * LANES, (hd + 1) * LANES)
        rq = jnp.where(in_aug, pltpu.roll(aug_q, shift, 1), 0.0)
        rk = jnp.where(in_aug, pltpu.roll(aug_k, shift, 1), 0.0)
        qa_ref[:, osl] = jnp.where(in_head, q[:, sl], rq).astype(BF16)
        ka_ref[:, osl] = jnp.where(in_head, k[:, sl], rk).astype(BF16)


def _in_proj(x2d, g, w_in_t, layer, b_f, *, seq, n_heads, tm):
    t, d = x2d.shape
    d_fox = n_heads * FOX_HEAD_DIM
    d_main = D_POOL + 3 * d_fox
    place, ones = _aug_constants(n_heads)
    kern = functools.partial(_in_proj_kernel, tiles_per_seq=seq // tm, n_heads=n_heads)
    row = lambda i: (i, 0)
    return pl.pallas_call(
        kern,
        grid=(t // tm,),
        in_specs=[pl.BlockSpec((tm, d), row), _const_spec(g.shape), _layer_spec(w_in_t, layer),
                  _const_spec(b_f.shape), _const_spec(place.shape), _const_spec(ones.shape)],
        out_specs=[pl.BlockSpec((tm, D_POOL), row), pl.BlockSpec((tm, n_heads * LANES), row),
                   pl.BlockSpec((tm, n_heads * LANES), row), pl.BlockSpec((tm, d_fox), row)],
        out_shape=[jax.ShapeDtypeStruct((t, D_POOL), F32),
                   jax.ShapeDtypeStruct((t, n_heads * LANES), BF16),
                   jax.ShapeDtypeStruct((t, n_heads * LANES), BF16),
                   jax.ShapeDtypeStruct((t, d_fox), BF16)],
        scratch_shapes=[pltpu.VMEM((1, LANES), F32), pltpu.VMEM((d, d_main), BF16),
                        pltpu.VMEM((LANES, d), BF16)],
        compiler_params=_params(1),
        name="in_proj",
    )(x2d, g, w_in_t, b_f, place, ones)


def _attn_kernel(q_ref, qn_ref, k_ref, v_ref, o_ref, m_ref, acc_ref, vtop_ref, vbot_ref, s_ref,
                 *, tk, nq):
    i = pl.program_id(2)
    tq = q_ref.shape[0]
    n_sub = tk // LANES

    def build_v():
        v = v_ref[...].astype(F32)
        lane = lax.broadcasted_iota(jnp.int32, v.shape, 1)
        vtop_ref[:, :LANES] = jnp.where(lane < FOX_HEAD_DIM, v, 0.0).astype(BF16)
        vtop_ref[:, LANES:] = jnp.where(lane == 0, 1.0, 0.0).astype(BF16)
        vbot_ref[:, :LANES] = jnp.where(lane >= FOX_HEAD_DIM, v, 0.0).astype(BF16)
        vbot_ref[:, LANES:] = jnp.where(lane == 1, 1.0, 0.0).astype(BF16)

    lane2 = lax.broadcasted_iota(jnp.int32, (tq, 2 * LANES), 1)
    even_lanes = (lane2 < FOX_HEAD_DIM) | (lane2 == LANES)

    def pv_of(ps, j):
        ks = pl.multiple_of(j * tk, tk)
        pv = jnp.dot(ps[0], vtop_ref[pl.ds(ks, tk), :], preferred_element_type=F32)
        return pv + jnp.dot(ps[1], vbot_ref[pl.ds(ks, tk), :], preferred_element_type=F32)

    def update_diagonal():
        tri = (lax.broadcasted_iota(jnp.int32, (LANES, LANES), 0)
               >= lax.broadcasted_iota(jnp.int32, (LANES, LANES), 1))
        ps = []
        for hh in range(2):
            s = s_ref[DIAG_SLOT, hh]
            cols = []
            for c in range(n_sub):
                lo, hi = c * LANES, (c + 1) * LANES
                parts = [jnp.full((lo, LANES), -jnp.inf, F32)] if c else []
                parts.append(jnp.where(tri, s[lo:hi, lo:hi], -jnp.inf))
                if hi < tq:
                    parts.append(s[hi:, lo:hi])
                cols.append(jnp.concatenate(parts, axis=0))
            best = functools.reduce(jnp.maximum, cols)
            m_next = jnp.broadcast_to(jnp.max(best, axis=1, keepdims=True), (tq, LANES))
            p_cols = []
            for c in range(n_sub):
                lo = c * LANES
                parts = [jnp.zeros((lo, LANES), F32)] if c else []
                parts.append(jnp.exp2(cols[c][lo:, :] - m_next[lo:, :]))
                p_cols.append(jnp.concatenate(parts, axis=0))
            ps.append(jnp.concatenate(p_cols, axis=1).astype(BF16))
            m_ref[hh] = m_next
        acc_ref[...] = pv_of(ps, i)

    def logits(j, slot, qr=q_ref):
        ks = pl.multiple_of(j * tk, tk)
        for hh in range(2):
            q = qr[:, hh * LANES:(hh + 1) * LANES]
            k = k_ref[pl.ds(ks, tk), hh * LANES:(hh + 1) * LANES]
            s_ref[slot, hh] = lax.dot_general(q, k, (((1,), (1,)), ((), ())),
                                              preferred_element_type=F32)

    def update(j, slot):
        ps, alphas = [], []
        for hh in range(2):
            s = s_ref[slot, hh]
            m_prev = m_ref[hh]
            m_next = jnp.maximum(m_prev, jnp.max(s, axis=1, keepdims=True))
            alphas.append(jnp.exp2(m_prev - m_next))
            ps.append(jnp.concatenate(
                [jnp.exp2(s[:, c * LANES:(c + 1) * LANES] - m_next) for c in range(n_sub)],
                axis=1).astype(BF16))
            m_ref[hh] = m_next
        alpha = jnp.where(even_lanes, jnp.concatenate([alphas[0]] * 2, axis=1),
                          jnp.concatenate([alphas[1]] * 2, axis=1))
        acc_ref[...] = alpha * acc_ref[...] + pv_of(ps, j)

    def finish():
        acc = acc_ref[...]
        l_e = jnp.broadcast_to(acc[:, LANES:LANES + 1], (tq, LANES))
        l_o = jnp.broadcast_to(acc[:, LANES + 1:LANES + 2], (tq, LANES))
        lane = lax.broadcasted_iota(jnp.int32, (tq, LANES), 1)
        o_ref[...] = (acc[:, :LANES] / jnp.where(lane < FOX_HEAD_DIM, l_e, l_o)).astype(o_ref.dtype)

    def next_diagonal():
        logits(jnp.minimum(i + 1, nq - 1), DIAG_SLOT, qn_ref)

    @pl.when(i == 0)
    def _():
        build_v()
        logits(0, DIAG_SLOT)
        update_diagonal()
        next_diagonal()
        finish()

    @pl.when(i > 0)
    def _():
        logits(i - 1, 1)
        update_diagonal()

        def step(c, slot):
            logits(c - 1, 1 - slot)
            update(c, slot)

        def two_steps(tt, carry):
            c = i - 1 - 2 * tt
            step(c, 1)
            step(c - 1, 0)
            return carry

        lax.fori_loop(0, lax.shift_right_logical(i - 1, 1), two_steps, 0)

        @pl.when(i % 2 == 0)
        def _():
            step(1, 1)
            next_diagonal()
            update(0, 0)
            finish()

        @pl.when(i % 2 == 1)
        def _():
            next_diagonal()
            update(0, 1)
            finish()


def _attention(qa, ka, v, *, batch, seq, n_heads, tq):
    t = qa.shape[0]
    n_pairs = n_heads // 2
    nq = seq // tq
    kern = functools.partial(_attn_kernel, tk=tq, nq=nq)
    return pl.pallas_call(
        kern,
        grid=(batch, n_pairs, nq),
        in_specs=[pl.BlockSpec((tq, 2 * LANES), lambda b, p, i: (b * nq + i, p)),
                  pl.BlockSpec((tq, 2 * LANES),
                               lambda b, p, i: (b * nq + jnp.minimum(i + 1, nq - 1), p)),
                  pl.BlockSpec((seq, 2 * LANES), lambda b, p, i: (b, p)),
                  pl.BlockSpec((seq, LANES), lambda b, p, i: (b, p))],
        out_specs=pl.BlockSpec((tq, LANES), lambda b, p, i: (b * nq + i, p)),
        out_shape=jax.ShapeDtypeStruct((t, n_pairs * LANES), BF16),
        scratch_shapes=[pltpu.VMEM((2, tq, LANES), F32), pltpu.VMEM((tq, 2 * LANES), F32),
                        pltpu.VMEM((seq, 2 * LANES), BF16), pltpu.VMEM((seq, 2 * LANES), BF16),
                        pltpu.VMEM((DIAG_SLOT + 1, 2, tq, tq), F32)],
        compiler_params=_params(3),
        name="fox_attn",
    )(qa, qa, ka, v)


def _mem_kv_kernel(mem_ref, g_ref, w_ref, kv_ref):
    h = _rms(mem_ref[...], g_ref[...]).astype(BF16)
    kv_ref[...] = jnp.dot(h, w_ref[...].astype(BF16), preferred_element_type=F32).astype(BF16)


def _mem_kv(mem2d, g, w_xkv, layer, *, batch):
    tmem = mem2d.shape[0] // batch
    d = mem2d.shape[1]
    n = w_xkv.shape[2]
    return pl.pallas_call(
        _mem_kv_kernel,
        grid=(batch,),
        in_specs=[pl.BlockSpec((tmem, d), lambda b: (b, 0)), _const_spec(g.shape),
                  _layer_spec(w_xkv, layer)],
        out_specs=pl.BlockSpec((tmem, n), lambda b: (b, 0)),
        out_shape=jax.ShapeDtypeStruct((mem2d.shape[0], n), BF16),
        compiler_params=_params(1),
        name="mem_kv",
    )(mem2d, g, w_xkv)


def _mix_kernel(x_ref, u_ref, up_ref, yf_ref, kv_ref, wp_ref, ps_ref, wm_ref,
                g_mpost_ref, g_xpre_ref, wq_ref, wo_ref, g_xpost_ref, wu_ref, wd_ref,
                o_ref, wu_bf_ref, wd_bf_ref, wbf_ref, *, tiles_per_seq):
    i = pl.program_id(0)
    tm, d = x_ref.shape
    seq_tile = i % tiles_per_seq

    wu_bf_ref[...] = wu_ref[...].astype(BF16)
    wd_bf_ref[...] = wd_ref[...].astype(BF16)

    @pl.when(i == 0)
    def _():
        for n, w_ref in enumerate((wm_ref, wq_ref, wo_ref)):
            wbf_ref[n] = w_ref[...].astype(BF16)

    halo = jnp.where(seq_tile == 0, 0.0, up_ref[...])
    ub = jnp.concatenate([halo, u_ref[...]], axis=0)
    s2 = ub + pltpu.roll(ub, 1, 0)
    s4 = s2 + pltpu.roll(s2, 2, 0)
    s8 = s4 + pltpu.roll(s4, 4, 0)
    s16 = s8 + pltpu.roll(s8, 8, 0)
    sums = dict(zip(POOL_WINDOWS, (s2, s4, s8, s16)))
    pos = seq_tile * tm + lax.broadcasted_iota(jnp.int32, (tm, D_POOL), 0) + 1
    lane = lax.broadcasted_iota(jnp.int32, (tm, D_POOL), 1)
    pooled = jnp.zeros((tm, D_POOL), F32)
    for g, w in enumerate(POOL_WINDOWS):
        cnt = jnp.minimum(pos, w).astype(F32)
        mean = sums[w][POOL_HALO:, :] / cnt
        in_group = (lane >= g * POOL_GROUP_DIM) & (lane < (g + 1) * POOL_GROUP_DIM)
        pooled = jnp.where(in_group, mean, pooled)
    diff = (pooled - u_ref[...]).astype(BF16)
    y_pool = jnp.dot(diff, wp_ref[...], preferred_element_type=F32) * ps_ref[...]

    y = jnp.dot(y_pool.astype(BF16), wbf_ref[0, :D_POOL, :], preferred_element_type=F32)
    y = y + jnp.dot(yf_ref[...], wbf_ref[0, D_POOL:, :], preferred_element_type=F32)
    x1 = x_ref[...] + _rms(y, g_mpost_ref[...])

    h = _rms(x1, g_xpre_ref[...]).astype(BF16)
    q = jnp.dot(h, wbf_ref[1], preferred_element_type=F32)
    dh = d // XA_HEADS
    outs = []
    for hd in range(XA_HEADS):
        qh = (q[:, hd * dh:(hd + 1) * dh] * (dh ** -0.5)).astype(BF16)
        kh = kv_ref[:, hd * dh:(hd + 1) * dh]
        vh = kv_ref[:, d + hd * dh:d + (hd + 1) * dh]
        s = lax.dot_general(qh, kh, (((1,), (1,)), ((), ())), preferred_element_type=F32)
        p = jnp.exp(s - jnp.max(s, axis=1, keepdims=True))
        o = jnp.dot(p.astype(BF16), vh, preferred_element_type=F32)
        outs.append((o / jnp.sum(p, axis=1, keepdims=True)).astype(BF16))
    att = jnp.concatenate(outs, axis=1)
    y2 = jnp.dot(att, wbf_ref[2], preferred_element_type=F32)
    o_ref[...] = x1 + _rms(y2, g_xpost_ref[...])


def _layer_spec(w, layer):
    return pl.BlockSpec((None,) + w.shape[1:], lambda *_: (layer,) + (0,) * (w.ndim - 1),
                        pipeline_mode=pl.Buffered(1))


def _mix(x2d, u, yf, kv, w_pool_bd, pool_scale, w_mix_out, g_mpost, g_xpre,
         w_xq, w_xo, g_xpost, w_up, w_down, layer, *, seq, tm):
    t, d = x2d.shape
    tps = seq // tm
    hb = tm // POOL_HALO
    tmem = kv.shape[0] // (t // seq)
    steps = t // tm
    up_cols, down_rows = w_up.shape[2] // steps, w_down.shape[1] // steps
    assert up_cols * steps == w_up.shape[2] and up_cols % LANES == 0
    assert down_rows * steps == w_down.shape[1] and down_rows % BF16_SUBLANES == 0
    kern = functools.partial(_mix_kernel, tiles_per_seq=tps)
    row = lambda i: (i, 0)
    cs = _const_spec
    return pl.pallas_call(
        kern,
        grid=(t // tm,),
        in_specs=[pl.BlockSpec((tm, d), row), pl.BlockSpec((tm, D_POOL), row),
                  pl.BlockSpec((POOL_HALO, D_POOL), lambda i: (jnp.maximum(i * hb - 1, 0), 0)),
                  pl.BlockSpec((tm, yf.shape[1]), row),
                  pl.BlockSpec((tmem, kv.shape[1]), lambda i: (i // tps, 0)),
                  cs(w_pool_bd.shape), cs(pool_scale.shape), _layer_spec(w_mix_out, layer),
                  cs(g_mpost.shape), cs(g_xpre.shape), _layer_spec(w_xq, layer),
                  _layer_spec(w_xo, layer), cs(g_xpost.shape),
                  pl.BlockSpec((None, w_up.shape[1], up_cols), lambda i: (layer, 0, i)),
                  pl.BlockSpec((None, down_rows, w_down.shape[2]), lambda i: (layer, i, 0))],
        out_specs=[pl.BlockSpec((tm, d), row),
                   pl.BlockSpec((w_up.shape[1], up_cols), lambda i: (0, i)),
                   pl.BlockSpec((down_rows, w_down.shape[2]), lambda i: (i, 0))],
        out_shape=[jax.ShapeDtypeStruct((t, d), F32),
                   jax.ShapeDtypeStruct(w_up.shape[1:], BF16),
                   jax.ShapeDtypeStruct(w_down.shape[1:], BF16)],
        scratch_shapes=[pltpu.VMEM((3, d, d), BF16)],
        compiler_params=_params(1),
        name="mix_xattn",
    )(x2d, u, u, yf, kv, w_pool_bd, pool_scale, w_mix_out, g_mpost, g_xpre, w_xq, w_xo, g_xpost,
      w_up, w_down)


def _gated_gelu(gate, up):
    a = -2.0 * LOG2E * math.sqrt(2.0 / math.pi)
    e = jnp.exp2(gate * (a + (a * 0.044715) * (gate * gate)))
    return (gate * up) / (1.0 + e)


def _ffn_kernel(x_ref, g_pre_ref, wu_ref, cw_ref, cb_ref, wd_ref, g_post_ref, o_ref, carry_ref,
                *, tiles_per_seq, ck):
    i = pl.program_id(0)
    tm = x_ref.shape[0]
    d_ff = wd_ref.shape[0]

    @pl.when(i % tiles_per_seq == 0)
    def _():
        carry_ref[...] = jnp.zeros_like(carry_ref)

    h = _rms(x_ref[...], g_pre_ref[...]).astype(BF16)

    def taps(xs, lo):
        out = xs * cw_ref[CONV_WIDTH - 1:CONV_WIDTH, lo:lo + ck]
        for tap in range(1, CONV_WIDTH):
            out = out + pltpu.roll(xs, tap, 0) * cw_ref[CONV_WIDTH - 1 - tap:CONV_WIDTH - tap, lo:lo + ck]
        return out

    def conv(c, which):
        lo = which * d_ff + c * ck
        hid = jnp.dot(h, wu_ref[:, lo:lo + ck], preferred_element_type=F32)
        prev = carry_ref[c, which]
        carry_ref[c, which] = hid[tm - CONV_CARRY:, :]
        head = taps(jnp.concatenate([prev, hid[:CONV_CARRY, :]], axis=0), lo)[CONV_CARRY:, :]
        body = taps(hid, lo)[CONV_CARRY:, :]
        return jnp.concatenate([head, body], axis=0) + cb_ref[:, lo:lo + ck]

    acc = jnp.zeros((tm, x_ref.shape[1]), F32)
    for c in range(d_ff // ck):
        act = _gated_gelu(conv(c, 0), conv(c, 1)).astype(BF16)
        acc = acc + jnp.dot(act, wd_ref[c * ck:(c + 1) * ck, :], preferred_element_type=F32)
    o_ref[...] = x_ref[...] + _rms(acc, g_post_ref[...])


def _ffn(x2d, g_pre, w_up, conv_w, conv_b, w_down, g_post, *, seq, tm, ck):
    t, d = x2d.shape
    d_ff = w_down.shape[0]
    kern = functools.partial(_ffn_kernel, tiles_per_seq=seq // tm, ck=ck)
    row = lambda i: (i, 0)
    consts = (g_pre, w_up, conv_w, conv_b, w_down, g_post)
    return pl.pallas_call(
        kern,
        grid=(t // tm,),
        in_specs=[pl.BlockSpec((tm, d), row)] + [_const_spec(a.shape) for a in consts],
        out_specs=pl.BlockSpec((tm, d), row),
        out_shape=jax.ShapeDtypeStruct((t, d), F32),
        scratch_shapes=[pltpu.VMEM((d_ff // ck, 2, CONV_CARRY, ck), F32)],
        compiler_params=_params(1),
        name="conv_ffn",
    )(x2d, *consts)


def _tile(n, pref):
    t = min(n, pref)
    assert n % t == 0, (n, t)
    return t


def kernel(x, mem, norm_mix_pre, norm_mix_post, w_in, b_forget, w_pool, pool_scale, w_mix_out,
           norm_mem, norm_xa_pre, norm_xa_post, w_xq, w_xkv, w_xo,
           norm_ffn_pre, norm_ffn_post, w_up, conv_w, conv_b, w_down):
    b, s, d = x.shape
    depth = w_in.shape[0]
    n_heads = b_forget.shape[1]
    d_fox = n_heads * FOX_HEAD_DIM
    d_main = D_POOL + 3 * d_fox
    assert n_heads % 2 == 0 and n_heads <= LANES
    assert w_in.shape[2] == d_main + n_heads
    tm = _tile(s, 512)
    tm_in = _tile(s, 1024)
    tq = _tile(s, 512)
    tm_ffn = _tile(s, 512)
    ck = _tile(w_down.shape[1], 4096)

    x2d = x.reshape(b * s, d)
    mem2d = mem.reshape(b * mem.shape[1], d)
    w_in_t = jnp.swapaxes(w_in, 1, 2)
    for l in range(depth):
        row = lambda a: a[l].reshape(1, -1)
        b_f = jnp.pad(b_forget[l], (0, LANES - n_heads)).reshape(1, LANES)
        w_pool_bd = jax.scipy.linalg.block_diag(*w_pool[l]).astype(BF16)

        u, qa, ka, v = _in_proj(x2d, row(norm_mix_pre), w_in_t, l, b_f,
                                seq=s, n_heads=n_heads, tm=tm_in)
        yf = _attention(qa, ka, v, batch=b, seq=s, n_heads=n_heads, tq=tq)
        kv = _mem_kv(mem2d, row(norm_mem), w_xkv, l, batch=b)
        x2d, w_up_bf, w_down_bf = _mix(
            x2d, u, yf, kv, w_pool_bd, pool_scale[l].reshape(1, -1), w_mix_out,
            row(norm_mix_post), row(norm_xa_pre), w_xq, w_xo, row(norm_xa_post), w_up, w_down, l,
            seq=s, tm=tm)
        x2d = _ffn(x2d, row(norm_ffn_pre), w_up_bf, conv_w[l].reshape(CONV_WIDTH, -1),
                   conv_b[l].reshape(1, -1), w_down_bf, row(norm_ffn_post),
                   seq=s, tm=tm_ffn, ck=ck)
    return x2d.reshape(b, s, d)
```

```python
import functools
import math

import jax
import jax.numpy as jnp
import numpy as np
from jax import lax
from jax.experimental import pallas as pl
from jax.experimental.pallas import tpu as pltpu

F32 = jnp.float32
BF16 = jnp.bfloat16

NORM_EPS = 1e-6
POOL_WINDOWS = (2, 4, 8, 16)
POOL_GROUP_DIM = 64
D_POOL = 256
FOX_HEAD_DIM = 64
XA_HEADS = 4
CONV_WIDTH = 3
LANES = 128
POOL_HALO = 16
CONV_CARRY = 8
BF16_SUBLANES = 16
AUG = 3
VMEM_LIMIT = 56 * 1024 * 1024
LOG2E = math.log2(math.e)
DIAG_SLOT = 2


def _rms(x, g):
    ms = jnp.mean(x * x, axis=-1, keepdims=True)
    return x * lax.rsqrt(ms + NORM_EPS) * g


def _const_spec(shape):
    nd = len(shape)
    return pl.BlockSpec(shape, lambda *_: (0,) * nd, pipeline_mode=pl.Buffered(1))


def _params(n_axes):
    return pltpu.CompilerParams(dimension_semantics=("arbitrary",) * n_axes,
                                vmem_limit_bytes=VMEM_LIMIT)


def _aug_constants(n_heads):
    place = np.zeros((AUG * LANES, 2 * LANES), np.float32)
    ones = np.zeros((8, 2 * LANES), np.float32)
    for h in range(n_heads):
        for n in range(AUG):
            place[n * LANES + h, 2 * AUG * h + n] = 1.0
            place[n * LANES + h, LANES + 2 * AUG * h + AUG + n] = -1.0
            ones[0, 2 * AUG * h + AUG + n] = 1.0
            ones[0, LANES + 2 * AUG * h + n] = 1.0
    return jnp.asarray(place, BF16), jnp.asarray(ones)


def _in_proj_kernel(x_ref, g_ref, w_ref, bf_ref, place_ref, ones_ref,
                    u_ref, qa_ref, ka_ref, v_ref, carry_ref, wm_ref, wf_ref,
                    *, tiles_per_seq, n_heads):
    i = pl.program_id(0)
    tm = x_ref.shape[0]
    d_fox = n_heads * FOX_HEAD_DIM
    d_main = D_POOL + 3 * d_fox

    @pl.when(i == 0)
    def _():
        blk = 2 * LANES
        for lo in range(0, d_main, blk):
            wm_ref[:, lo:lo + blk] = w_ref[lo:lo + blk, :].T.astype(BF16)
        wf_ref[...] = jnp.zeros_like(wf_ref)
        wf_ref[:n_heads, :] = w_ref[d_main:d_main + n_heads, :].astype(BF16)

    @pl.when(i % tiles_per_seq == 0)
    def _():
        carry_ref[...] = jnp.zeros_like(carry_ref)

    h = _rms(x_ref[...], g_ref[...]).astype(BF16)

    f = lax.dot_general(h, wf_ref[...], (((1,), (1,)), ((), ())),
                        preferred_element_type=F32) + bf_ref[...]
    c = jnp.minimum(f, 0.0) - jnp.log1p(jnp.exp(-jnp.abs(f)))
    row = lax.broadcasted_iota(jnp.int32, c.shape, 0)
    d = 1
    while d < tm:
        c = c + jnp.where(row >= d, pltpu.roll(c, d, 0), 0.0)
        d *= 2
    c = c + carry_ref[...]
    carry_ref[...] = c[tm - 1:tm, :]

    c2 = c * LOG2E
    hi = c2.astype(BF16)
    r1 = c2 - hi.astype(F32)
    mid = r1.astype(BF16)
    lo = (r1 - mid.astype(F32)).astype(BF16)
    pieces = jnp.concatenate([hi, mid, lo], axis=1)
    aug = jnp.dot(pieces, place_ref[...], preferred_element_type=F32) + ones_ref[0:1, :]
    aug_q, aug_k = aug[:, :LANES], aug[:, LANES:]

    main = jnp.dot(h, wm_ref[...], preferred_element_type=F32)
    u_ref[...] = main[:, :D_POOL]
    q = main[:, D_POOL:D_POOL + d_fox] * (FOX_HEAD_DIM ** -0.5 * LOG2E)
    k = main[:, D_POOL + d_fox:D_POOL + 2 * d_fox]
    v_ref[...] = main[:, D_POOL + 2 * d_fox:D_POOL + 3 * d_fox].astype(BF16)

    lane = lax.broadcasted_iota(jnp.int32, (tm, LANES), 1)
    for hd in range(n_heads):
        pair, odd = divmod(hd, 2)
        qk_lo = odd * FOX_HEAD_DIM
        a0 = (1 - odd) * FOX_HEAD_DIM
        shift = (a0 - 2 * AUG * hd) % LANES
        in_head = (lane >= qk_lo) & (lane < qk_lo + FOX_HEAD_DIM)
        in_aug = (lane >= a0) & (lane < a0 + 2 * AUG)
        sl = slice(pair * LANES, (pair + 1) * LANES)
        osl = slice(hd * LANES, (hd + 1) * LANES)
        rq = jnp.where(in_aug, pltpu.roll(aug_q, shift, 1), 0.0)
        rk = jnp.where(in_aug, pltpu.roll(aug_k, shift, 1), 0.0)
        qa_ref[:, osl] = jnp.where(in_head, q[:, sl], rq).astype(BF16)
        ka_ref[:, osl] = jnp.where(in_head, k[:, sl], rk).astype(BF16)


def _in_proj(x2d, g, w_in_t, layer, b_f, *, seq, n_heads, tm):
    t, d = x2d.shape
    d_fox = n_heads * FOX_HEAD_DIM
    d_main = D_POOL + 3 * d_fox
    place, ones = _aug_constants(n_heads)
    kern = functools.partial(_in_proj_kernel, tiles_per_seq=seq // tm, n_heads=n_heads)
    row = lambda i: (i, 0)
    return pl.pallas_call(
        kern,
        grid=(t // tm,),
        in_specs=[pl.BlockSpec((tm, d), row), _const_spec(g.shape), _layer_spec(w_in_t, layer),
                  _const_spec(b_f.shape), _const_spec(place.shape), _const_spec(ones.shape)],
        out_specs=[pl.BlockSpec((tm, D_POOL), row), pl.BlockSpec((tm, n_heads * LANES), row),
                   pl.BlockSpec((tm, n_heads * LANES), row), pl.BlockSpec((tm, d_fox), row)],
        out_shape=[jax.ShapeDtypeStruct((t, D_POOL), F32),
                   jax.ShapeDtypeStruct((t, n_heads * LANES), BF16),
                   jax.ShapeDtypeStruct((t, n_heads * LANES), BF16),
                   jax.ShapeDtypeStruct((t, d_fox), BF16)],
        scratch_shapes=[pltpu.VMEM((1, LANES), F32), pltpu.VMEM((d, d_main), BF16),
                        pltpu.VMEM((LANES, d), BF16)],
        compiler_params=_params(1),
        name="in_proj",
    )(x2d, g, w_in_t, b_f, place, ones)


def _attn_kernel(q_ref, qn_ref, k_ref, v_ref, o_ref, m_ref, acc_ref, vtop_ref, vbot_ref, s_ref,
                 *, tk, nq):
    i = pl.program_id(2)
    tq = q_ref.shape[0]
    n_sub = tk // LANES

    def build_v():
        v = v_ref[...].astype(F32)
        lane = lax.broadcasted_iota(jnp.int32, v.shape, 1)
        vtop_ref[:, :LANES] = jnp.where(lane < FOX_HEAD_DIM, v, 0.0).astype(BF16)
        vtop_ref[:, LANES:] = jnp.where(lane == 0, 1.0, 0.0).astype(BF16)
        vbot_ref[:, :LANES] = jnp.where(lane >= FOX_HEAD_DIM, v, 0.0).astype(BF16)
        vbot_ref[:, LANES:] = jnp.where(lane == 1, 1.0, 0.0).astype(BF16)

    lane2 = lax.broadcasted_iota(jnp.int32, (tq, 2 * LANES), 1)
    even_lanes = (lane2 < FOX_HEAD_DIM) | (lane2 == LANES)

    def pv_of(ps, j):
        ks = pl.multiple_of(j * tk, tk)
        pv = jnp.dot(ps[0], vtop_ref[pl.ds(ks, tk), :], preferred_element_type=F32)
        return pv + jnp.dot(ps[1], vbot_ref[pl.ds(ks, tk), :], preferred_element_type=F32)

    def update_diagonal(k):
        tri = (lax.broadcasted_iota(jnp.int32, (LANES, LANES), 0)
               >= lax.broadcasted_iota(jnp.int32, (LANES, LANES), 1))
        ps = []
        for hh in range(2):
            s = s_ref[DIAG_SLOT, hh]
            cols = []
            for c in range(n_sub):
                lo, hi = c * LANES, (c + 1) * LANES
                parts = [jnp.full((lo, LANES), -jnp.inf, F32)] if c else []
                parts.append(jnp.where(tri, s[lo:hi, lo:hi], -jnp.inf))
                if hi < tq:
                    parts.append(s[hi:, lo:hi])
                cols.append(jnp.concatenate(parts, axis=0))
            best = functools.reduce(jnp.maximum, cols)
            m_next = jnp.broadcast_to(jnp.max(best, axis=1, keepdims=True), (tq, LANES))
            p_cols = []
            for c in range(n_sub):
                lo = c * LANES
                parts = [jnp.zeros((lo, LANES), F32)] if c else []
                parts.append(jnp.exp2(cols[c][lo:, :] - m_next[lo:, :]))
                p_cols.append(jnp.concatenate(parts, axis=0))
            ps.append(jnp.concatenate(p_cols, axis=1).astype(BF16))
            m_ref[hh] = m_next
        acc_ref[...] = pv_of(ps, k)

    def logits(j, slot, qr=q_ref):
        ks = pl.multiple_of(j * tk, tk)
        for hh in range(2):
            q = qr[:, hh * LANES:(hh + 1) * LANES]
            k = k_ref[pl.ds(ks, tk), hh * LANES:(hh + 1) * LANES]
            s_ref[slot, hh] = lax.dot_general(q, k, (((1,), (1,)), ((), ())),
                                              preferred_element_type=F32)

    def update(j, slot):
        ps, alphas = [], []
        for hh in range(2):
            s = s_ref[slot, hh]
            m_prev = m_ref[hh]
            m_next = jnp.maximum(m_prev, jnp.max(s, axis=1, keepdims=True))
            alphas.append(jnp.exp2(m_prev - m_next))
            ps.append(jnp.concatenate(
                [jnp.exp2(s[:, c * LANES:(c + 1) * LANES] - m_next) for c in range(n_sub)],
                axis=1).astype(BF16))
            m_ref[hh] = m_next
        alpha = jnp.where(even_lanes, jnp.concatenate([alphas[0]] * 2, axis=1),
                          jnp.concatenate([alphas[1]] * 2, axis=1))
        acc_ref[...] = alpha * acc_ref[...] + pv_of(ps, j)

    def finish():
        acc = acc_ref[...]
        l_e = jnp.broadcast_to(acc[:, LANES:LANES + 1], (tq, LANES))
        l_o = jnp.broadcast_to(acc[:, LANES + 1:LANES + 2], (tq, LANES))
        lane = lax.broadcasted_iota(jnp.int32, (tq, LANES), 1)
        o_ref[...] = (acc[:, :LANES] / jnp.where(lane < FOX_HEAD_DIM, l_e, l_o)).astype(o_ref.dtype)

    def tile(k):
        if k == 0:
            build_v()
            logits(0, DIAG_SLOT)
        else:
            logits(k - 1, 1)
        update_diagonal(k)
        slot = 1
        for c in range(k - 1, -1, -1):
            if c > 0:
                logits(c - 1, 1 - slot)
            elif k + 1 < nq:
                logits(k + 1, DIAG_SLOT, qn_ref)
            update(c, slot)
            slot = 1 - slot
        if k == 0 and nq > 1:
            logits(1, DIAG_SLOT, qn_ref)
        finish()

    for k in range(nq):
        pl.when(i == k)(functools.partial(tile, k))


def _attention(qa, ka, v, *, batch, seq, n_heads, tq):
    t = qa.shape[0]
    n_pairs = n_heads // 2
    nq = seq // tq
    kern = functools.partial(_attn_kernel, tk=tq, nq=nq)
    return pl.pallas_call(
        kern,
        grid=(batch, n_pairs, nq),
        in_specs=[pl.BlockSpec((tq, 2 * LANES), lambda b, p, i: (b * nq + i, p)),
                  pl.BlockSpec((tq, 2 * LANES),
                               lambda b, p, i: (b * nq + jnp.minimum(i + 1, nq - 1), p)),
                  pl.BlockSpec((seq, 2 * LANES), lambda b, p, i: (b, p)),
                  pl.BlockSpec((seq, LANES), lambda b, p, i: (b, p))],
        out_specs=pl.BlockSpec((tq, LANES), lambda b, p, i: (b * nq + i, p)),
        out_shape=jax.ShapeDtypeStruct((t, n_pairs * LANES), BF16),
        scratch_shapes=[pltpu.VMEM((2, tq, LANES), F32), pltpu.VMEM((tq, 2 * LANES), F32),
                        pltpu.VMEM((seq, 2 * LANES), BF16), pltpu.VMEM((seq, 2 * LANES), BF16),
                        pltpu.VMEM((DIAG_SLOT + 1, 2, tq, tq), F32)],
        compiler_params=_params(3),
        name="fox_attn",
    )(qa, qa, ka, v)


def _mem_kv_kernel(mem_ref, g_ref, w_ref, kv_ref):
    h = _rms(mem_ref[...], g_ref[...]).astype(BF16)
    kv_ref[...] = jnp.dot(h, w_ref[...].astype(BF16), preferred_element_type=F32).astype(BF16)


def _mem_kv(mem2d, g, w_xkv, layer, *, batch):
    tmem = mem2d.shape[0] // batch
    d = mem2d.shape[1]
    n = w_xkv.shape[2]
    return pl.pallas_call(
        _mem_kv_kernel,
        grid=(batch,),
        in_specs=[pl.BlockSpec((tmem, d), lambda b: (b, 0)), _const_spec(g.shape),
                  _layer_spec(w_xkv, layer)],
        out_specs=pl.BlockSpec((tmem, n), lambda b: (b, 0)),
        out_shape=jax.ShapeDtypeStruct((mem2d.shape[0], n), BF16),
        compiler_params=_params(1),
        name="mem_kv",
    )(mem2d, g, w_xkv)


def _mix_kernel(x_ref, u_ref, up_ref, yf_ref, kv_ref, wp_ref, ps_ref, wm_ref,
                g_mpost_ref, g_xpre_ref, wq_ref, wo_ref, g_xpost_ref, wu_ref, wd_ref,
                o_ref, wu_bf_ref, wd_bf_ref, wbf_ref, *, tiles_per_seq):
    i = pl.program_id(0)
    tm, d = x_ref.shape
    seq_tile = i % tiles_per_seq

    wu_bf_ref[...] = wu_ref[...].astype(BF16)
    wd_bf_ref[...] = wd_ref[...].astype(BF16)

    @pl.when(i == 0)
    def _():
        for n, w_ref in enumerate((wm_ref, wq_ref, wo_ref)):
            wbf_ref[n] = w_ref[...].astype(BF16)

    halo = jnp.where(seq_tile == 0, 0.0, up_ref[...])
    ub = jnp.concatenate([halo, u_ref[...]], axis=0)
    s2 = ub + pltpu.roll(ub, 1, 0)
    s4 = s2 + pltpu.roll(s2, 2, 0)
    s8 = s4 + pltpu.roll(s4, 4, 0)
    s16 = s8 + pltpu.roll(s8, 8, 0)
    sums = dict(zip(POOL_WINDOWS, (s2, s4, s8, s16)))
    pos = seq_tile * tm + lax.broadcasted_iota(jnp.int32, (tm, D_POOL), 0) + 1
    lane = lax.broadcasted_iota(jnp.int32, (tm, D_POOL), 1)
    pooled = jnp.zeros((tm, D_POOL), F32)
    for g, w in enumerate(POOL_WINDOWS):
        cnt = jnp.minimum(pos, w).astype(F32)
        mean = sums[w][POOL_HALO:, :] / cnt
        in_group = (lane >= g * POOL_GROUP_DIM) & (lane < (g + 1) * POOL_GROUP_DIM)
        pooled = jnp.where(in_group, mean, pooled)
    diff = (pooled - u_ref[...]).astype(BF16)
    y_pool = jnp.dot(diff, wp_ref[...], preferred_element_type=F32) * ps_ref[...]

    y = jnp.dot(y_pool.astype(BF16), wbf_ref[0, :D_POOL, :], preferred_element_type=F32)
    y = y + jnp.dot(yf_ref[...], wbf_ref[0, D_POOL:, :], preferred_element_type=F32)
    x1 = x_ref[...] + _rms(y, g_mpost_ref[...])

    h = _rms(x1, g_xpre_ref[...]).astype(BF16)
    q = jnp.dot(h, wbf_ref[1], preferred_element_type=F32)
    dh = d // XA_HEADS
    outs = []
    for hd in range(XA_HEADS):
        qh = (q[:, hd * dh:(hd + 1) * dh] * (dh ** -0.5)).astype(BF16)
        kh = kv_ref[:, hd * dh:(hd + 1) * dh]
        vh = kv_ref[:, d + hd * dh:d + (hd + 1) * dh]
        s = lax.dot_general(qh, kh, (((1,), (1,)), ((), ())), preferred_element_type=F32)
        p = jnp.exp(s - jnp.max(s, axis=1, keepdims=True))
        o = jnp.dot(p.astype(BF16), vh, preferred_element_type=F32)
        outs.append((o / jnp.sum(p, axis=1, keepdims=True)).astype(BF16))
    att = jnp.concatenate(outs, axis=1)
    y2 = jnp.dot(att, wbf_ref[2], preferred_element_type=F32)
    o_ref[...] = x1 + _rms(y2, g_xpost_ref[...])


def _layer_spec(w, layer):
    return pl.BlockSpec((None,) + w.shape[1:], lambda *_: (layer,) + (0,) * (w.ndim - 1),
                        pipeline_mode=pl.Buffered(1))


def _mix(x2d, u, yf, kv, w_pool_bd, pool_scale, w_mix_out, g_mpost, g_xpre,
         w_xq, w_xo, g_xpost, w_up, w_down, layer, *, seq, tm):
    t, d = x2d.shape
    tps = seq // tm
    hb = tm // POOL_HALO
    tmem = kv.shape[0] // (t // seq)
    steps = t // tm
    up_cols, down_rows = w_up.shape[2] // steps, w_down.shape[1] // steps
    assert up_cols * steps == w_up.shape[2] and up_cols % LANES == 0
    assert down_rows * steps == w_down.shape[1] and down_rows % BF16_SUBLANES == 0
    kern = functools.partial(_mix_kernel, tiles_per_seq=tps)
    row = lambda i: (i, 0)
    cs = _const_spec
    return pl.pallas_call(
        kern,
        grid=(t // tm,),
        in_specs=[pl.BlockSpec((tm, d), row), pl.BlockSpec((tm, D_POOL), row),
                  pl.BlockSpec((POOL_HALO, D_POOL), lambda i: (jnp.maximum(i * hb - 1, 0), 0)),
                  pl.BlockSpec((tm, yf.shape[1]), row),
                  pl.BlockSpec((tmem, kv.shape[1]), lambda i: (i // tps, 0)),
                  cs(w_pool_bd.shape), cs(pool_scale.shape), _layer_spec(w_mix_out, layer),
                  cs(g_mpost.shape), cs(g_xpre.shape), _layer_spec(w_xq, layer),
                  _layer_spec(w_xo, layer), cs(g_xpost.shape),
                  pl.BlockSpec((None, w_up.shape[1], up_cols), lambda i: (layer, 0, i)),
                  pl.BlockSpec((None, down_rows, w_down.shape[2]), lambda i: (layer, i, 0))],
        out_specs=[pl.BlockSpec((tm, d), row),
                   pl.BlockSpec((w_up.shape[1], up_cols), lambda i: (0, i)),
                   pl.BlockSpec((down_rows, w_down.shape[2]), lambda i: (i, 0))],
        out_shape=[jax.ShapeDtypeStruct((t, d), F32),
                   jax.ShapeDtypeStruct(w_up.shape[1:], BF16),
                   jax.ShapeDtypeStruct(w_down.shape[1:], BF16)],
        scratch_shapes=[pltpu.VMEM((3, d, d), BF16)],
        compiler_params=_params(1),
        name="mix_xattn",
    )(x2d, u, u, yf, kv, w_pool_bd, pool_scale, w_mix_out, g_mpost, g_xpre, w_xq, w_xo, g_xpost,
      w_up, w_down)


def _gated_gelu(gate, up):
    a = -2.0 * LOG2E * math.sqrt(2.0 / math.pi)
    e = jnp.exp2(gate * (a + (a * 0.044715) * (gate * gate)))
    return (gate * up) / (1.0 + e)


def _ffn_kernel(x_ref, g_pre_ref, wu_ref, cw_ref, cb_ref, wd_ref, g_post_ref, o_ref, carry_ref,
                *, tiles_per_seq, ck):
    i = pl.program_id(0)
    tm = x_ref.shape[0]
    d_ff = wd_ref.shape[0]

    @pl.when(i % tiles_per_seq == 0)
    def _():
        carry_ref[...] = jnp.zeros_like(carry_ref)

    h = _rms(x_ref[...], g_pre_ref[...]).astype(BF16)

    def taps(xs, lo):
        out = xs * cw_ref[CONV_WIDTH - 1:CONV_WIDTH, lo:lo + ck]
        for tap in range(1, CONV_WIDTH):
            out = out + pltpu.roll(xs, tap, 0) * cw_ref[CONV_WIDTH - 1 - tap:CONV_WIDTH - tap, lo:lo + ck]
        return out

    def conv(c, which):
        lo = which * d_ff + c * ck
        hid = jnp.dot(h, wu_ref[:, lo:lo + ck], preferred_element_type=F32)
        prev = carry_ref[c, which]
        carry_ref[c, which] = hid[tm - CONV_CARRY:, :]
        head = taps(jnp.concatenate([prev, hid[:CONV_CARRY, :]], axis=0), lo)[CONV_CARRY:, :]
        body = taps(hid, lo)[CONV_CARRY:, :]
        return jnp.concatenate([head, body], axis=0) + cb_ref[:, lo:lo + ck]

    acc = jnp.zeros((tm, x_ref.shape[1]), F32)
    for c in range(d_ff // ck):
        act = _gated_gelu(conv(c, 0), conv(c, 1)).astype(BF16)
        acc = acc + jnp.dot(act, wd_ref[c * ck:(c + 1) * ck, :], preferred_element_type=F32)
    o_ref[...] = x_ref[...] + _rms(acc, g_post_ref[...])


def _ffn(x2d, g_pre, w_up, conv_w, conv_b, w_down, g_post, *, seq, tm, ck):
    t, d = x2d.shape
    d_ff = w_down.shape[0]
    kern = functools.partial(_ffn_kernel, tiles_per_seq=seq // tm, ck=ck)
    row = lambda i: (i, 0)
    consts = (g_pre, w_up, conv_w, conv_b, w_down, g_post)
    return pl.pallas_call(
        kern,
        grid=(t // tm,),
        in_specs=[pl.BlockSpec((tm, d), row)] + [_const_spec(a.shape) for a in consts],
        out_specs=pl.BlockSpec((tm, d), row),
        out_shape=jax.ShapeDtypeStruct((t, d), F32),
        scratch_shapes=[pltpu.VMEM((d_ff // ck, 2, CONV_CARRY, ck), F32)],
        compiler_params=_params(1),
        name="conv_ffn",
    )(x2d, *consts)


def _tile(n, pref):
    t = min(n, pref)
    assert n % t == 0, (n, t)
    return t


def kernel(x, mem, norm_mix_pre, norm_mix_post, w_in, b_forget, w_pool, pool_scale, w_mix_out,
           norm_mem, norm_xa_pre, norm_xa_post, w_xq, w_xkv, w_xo,
           norm_ffn_pre, norm_ffn_post, w_up, conv_w, conv_b, w_down):
    b, s, d = x.shape
    depth = w_in.shape[0]
    n_heads = b_forget.shape[1]
    d_fox = n_heads * FOX_HEAD_DIM
    d_main = D_POOL + 3 * d_fox
    assert n_heads % 2 == 0 and n_heads <= LANES
    assert w_in.shape[2] == d_main + n_heads
    tm = _tile(s, 512)
    tm_in = _tile(s, 1024)
    tq = _tile(s, 512)
    tm_ffn = _tile(s, 512)
    ck = _tile(w_down.shape[1], 4096)

    x2d = x.reshape(b * s, d)
    mem2d = mem.reshape(b * mem.shape[1], d)
    w_in_t = jnp.swapaxes(w_in, 1, 2)
    for l in range(depth):
        row = lambda a: a[l].reshape(1, -1)
        b_f = jnp.pad(b_forget[l], (0, LANES - n_heads)).reshape(1, LANES)
        w_pool_bd = jax.scipy.linalg.block_diag(*w_pool[l]).astype(BF16)

        u, qa, ka, v = _in_proj(x2d, row(norm_mix_pre), w_in_t, l, b_f,
                                seq=s, n_heads=n_heads, tm=tm_in)
        yf = _attention(qa, ka, v, batch=b, seq=s, n_heads=n_heads, tq=tq)
        kv = _mem_kv(mem2d, row(norm_mem), w_xkv, l, batch=b)
        x2d, w_up_bf, w_down_bf = _mix(
            x2d, u, yf, kv, w_pool_bd, pool_scale[l].reshape(1, -1), w_mix_out,
            row(norm_mix_post), row(norm_xa_pre), w_xq, w_xo, row(norm_xa_post), w_up, w_down, l,
            seq=s, tm=tm)
        x2d = _ffn(x2d, row(norm_ffn_pre), w_up_bf, conv_w[l].reshape(CONV_WIDTH, -1),
                   conv_b[l].reshape(1, -1), w_down_bf, row(norm_ffn_post),
                   seq=s, tm=tm_ffn, ck=ck)
    return x2d.reshape(b, s, d)
```

```python
import functools
import math

import jax
import jax.numpy as jnp
import numpy as np
from jax import lax
from jax.experimental import pallas as pl
from jax.experimental.pallas import tpu as pltpu

F32 = jnp.float32
BF16 = jnp.bfloat16

NORM_EPS = 1e-6
POOL_WINDOWS = (2, 4, 8, 16)
POOL_GROUP_DIM = 64
D_POOL = 256
FOX_HEAD_DIM = 64
XA_HEADS = 4
CONV_WIDTH = 3
LANES = 128
POOL_HALO = 16
CONV_CARRY = 8
BF16_SUBLANES = 16
AUG = 3
VMEM_LIMIT = 56 * 1024 * 1024
LOG2E = math.log2(math.e)
DIAG_SLOT = 2


def _rms(x, g):
    ms = jnp.mean(x * x, axis=-1, keepdims=True)
    return x * lax.rsqrt(ms + NORM_EPS) * g


def _const_spec(shape):
    nd = len(shape)
    return pl.BlockSpec(shape, lambda *_: (0,) * nd, pipeline_mode=pl.Buffered(1))


def _layer_spec(w, layer):
    return pl.BlockSpec((None,) + w.shape[1:], lambda *_: (layer,) + (0,) * (w.ndim - 1),
                        pipeline_mode=pl.Buffered(1))


def _params(n_axes):
    return pltpu.CompilerParams(dimension_semantics=("arbitrary",) * n_axes,
                                vmem_limit_bytes=VMEM_LIMIT)


def _aug_constants(n_heads):
    place = np.zeros((AUG * LANES, 2 * LANES), np.float32)
    ones = np.zeros((8, 2 * LANES), np.float32)
    for h in range(n_heads):
        for n in range(AUG):
            place[n * LANES + h, 2 * AUG * h + n] = 1.0
            place[n * LANES + h, LANES + 2 * AUG * h + AUG + n] = -1.0
            ones[0, 2 * AUG * h + AUG + n] = 1.0
            ones[0, LANES + 2 * AUG * h + n] = 1.0
    return jnp.asarray(place, BF16), jnp.asarray(ones)


def _in_proj_kernel(x_ref, g_ref, w_ref, bf_ref, place_ref, ones_ref,
                    u_ref, qa_ref, ka_ref, v_ref, carry_ref, wm_ref, wf_ref,
                    *, tiles_per_seq, n_heads):
    i = pl.program_id(0)
    tm = x_ref.shape[0]
    d_fox = n_heads * FOX_HEAD_DIM
    d_main = D_POOL + 3 * d_fox

    @pl.when(i == 0)
    def _():
        blk = 2 * LANES
        for lo in range(0, d_main, blk):
            wm_ref[:, lo:lo + blk] = w_ref[lo:lo + blk, :].T.astype(BF16)
        wf_ref[...] = jnp.zeros_like(wf_ref)
        wf_ref[:n_heads, :] = w_ref[d_main:d_main + n_heads, :].astype(BF16)

    @pl.when(i % tiles_per_seq == 0)
    def _():
        carry_ref[...] = jnp.zeros_like(carry_ref)

    h = _rms(x_ref[...], g_ref[...]).astype(BF16)

    f = lax.dot_general(h, wf_ref[...], (((1,), (1,)), ((), ())),
                        preferred_element_type=F32) + bf_ref[...]
    c = jnp.minimum(f, 0.0) - jnp.log1p(jnp.exp(-jnp.abs(f)))
    row = lax.broadcasted_iota(jnp.int32, c.shape, 0)
    d = 1
    while d < tm:
        c = c + jnp.where(row >= d, pltpu.roll(c, d, 0), 0.0)
        d *= 2
    c = c + carry_ref[...]
    carry_ref[...] = c[tm - 1:tm, :]

    c2 = c * LOG2E
    hi = c2.astype(BF16)
    r1 = c2 - hi.astype(F32)
    mid = r1.astype(BF16)
    lo = (r1 - mid.astype(F32)).astype(BF16)
    pieces = jnp.concatenate([hi, mid, lo], axis=1)
    aug = jnp.dot(pieces, place_ref[...], preferred_element_type=F32) + ones_ref[0:1, :]
    aug_q, aug_k = aug[:, :LANES], aug[:, LANES:]

    main = jnp.dot(h, wm_ref[...], preferred_element_type=F32)
    u_ref[...] = main[:, :D_POOL]
    q = main[:, D_POOL:D_POOL + d_fox] * (FOX_HEAD_DIM ** -0.5 * LOG2E)
    k = main[:, D_POOL + d_fox:D_POOL + 2 * d_fox]
    v_ref[...] = main[:, D_POOL + 2 * d_fox:D_POOL + 3 * d_fox].astype(BF16)

    lane = lax.broadcasted_iota(jnp.int32, (tm, LANES), 1)
    for hd in range(n_heads):
        pair, odd = divmod(hd, 2)
        qk_lo = odd * FOX_HEAD_DIM
        a0 = (1 - odd) * FOX_HEAD_DIM
        shift = (a0 - 2 * AUG * hd) % LANES
        in_head = (lane >= qk_lo) & (lane < qk_lo + FOX_HEAD_DIM)
        in_aug = (lane >= a0) & (lane < a0 + 2 * AUG)
        sl = slice(pair * LANES, (pair + 1) * LANES)
        osl = slice(hd * LANES, (hd + 1) * LANES)
        rq = jnp.where(in_aug, pltpu.roll(aug_q, shift, 1), 0.0)
        rk = jnp.where(in_aug, pltpu.roll(aug_k, shift, 1), 0.0)
        qa_ref[:, osl] = jnp.where(in_head, q[:, sl], rq).astype(BF16)
        ka_ref[:, osl] = jnp.where(in_head, k[:, sl], rk).astype(BF16)


def _in_proj(x2d, g, w_in_t, layer, b_f, *, seq, n_heads, tm):
    t, d = x2d.shape
    d_fox = n_heads * FOX_HEAD_DIM
    d_main = D_POOL + 3 * d_fox
    place, ones = _aug_constants(n_heads)
    kern = functools.partial(_in_proj_kernel, tiles_per_seq=seq // tm, n_heads=n_heads)
    row = lambda i: (i, 0)
    return pl.pallas_call(
        kern,
        grid=(t // tm,),
        in_specs=[pl.BlockSpec((tm, d), row), _const_spec(g.shape), _layer_spec(w_in_t, layer),
                  _const_spec(b_f.shape), _const_spec(place.shape), _const_spec(ones.shape)],
        out_specs=[pl.BlockSpec((tm, D_POOL), row), pl.BlockSpec((tm, n_heads * LANES), row),
                   pl.BlockSpec((tm, n_heads * LANES), row), pl.BlockSpec((tm, d_fox), row)],
        out_shape=[jax.ShapeDtypeStruct((t, D_POOL), F32),
                   jax.ShapeDtypeStruct((t, n_heads * LANES), BF16),
                   jax.ShapeDtypeStruct((t, n_heads * LANES), BF16),
                   jax.ShapeDtypeStruct((t, d_fox), BF16)],
        scratch_shapes=[pltpu.VMEM((1, LANES), F32), pltpu.VMEM((d, d_main), BF16),
                        pltpu.VMEM((LANES, d), BF16)],
        compiler_params=_params(1),
        name="in_proj",
    )(x2d, g, w_in_t, b_f, place, ones)


def _attn_kernel(q_ref, qn_ref, k_ref, v_ref, o_ref, m_ref, acc_ref, vtop_ref, vbot_ref, s_ref,
                 *, tq, n_steps):
    i = pl.program_id(2)
    tk = tq
    tiles = q_ref.shape[0] // tq
    n_sub = tk // LANES
    half = tq // 2
    nt = (((1,), (1,)), ((), ()))

    def build_v():
        v = v_ref[...].astype(F32)
        lane = lax.broadcasted_iota(jnp.int32, v.shape, 1)
        vtop_ref[:, :LANES] = jnp.where(lane < FOX_HEAD_DIM, v, 0.0).astype(BF16)
        vtop_ref[:, LANES:] = jnp.where(lane == 0, 1.0, 0.0).astype(BF16)
        vbot_ref[:, :LANES] = jnp.where(lane >= FOX_HEAD_DIM, v, 0.0).astype(BF16)
        vbot_ref[:, LANES:] = jnp.where(lane == 1, 1.0, 0.0).astype(BF16)

    lane2 = lax.broadcasted_iota(jnp.int32, (tq, 2 * LANES), 1)
    even_lanes = (lane2 < FOX_HEAD_DIM) | (lane2 == LANES)

    def pv_of(ps, key0, n_keys):
        pv = jnp.dot(ps[0], vtop_ref[pl.ds(key0, n_keys), :], preferred_element_type=F32)
        return pv + jnp.dot(ps[1], vbot_ref[pl.ds(key0, n_keys), :], preferred_element_type=F32)

    def logits_diagonal(k, q, s):
        for hh in range(2):
            qh = q[:, hh * LANES:(hh + 1) * LANES]
            kh = k_ref[pl.ds(k * tk, tk), hh * LANES:(hh + 1) * LANES]
            s[DIAG_SLOT, hh, :half, :half] = lax.dot_general(qh[:half], kh[:half], nt,
                                                            preferred_element_type=F32)
            s[DIAG_SLOT, hh, half:, :] = lax.dot_general(qh[half:], kh, nt,
                                                         preferred_element_type=F32)

    def logits(c, slot, q, s):
        for hh in range(2):
            qh = q[:, hh * LANES:(hh + 1) * LANES]
            kh = k_ref[pl.ds(c * tk, tk), hh * LANES:(hh + 1) * LANES]
            s[slot, hh] = lax.dot_general(qh, kh, nt, preferred_element_type=F32)

    def update_diagonal(k, m, acc, s):
        tri = (lax.broadcasted_iota(jnp.int32, (LANES, LANES), 0)
               >= lax.broadcasted_iota(jnp.int32, (LANES, LANES), 1))
        ps = []
        for hh in range(2):
            sd = s.at[DIAG_SLOT, hh]
            cols = []
            for c in range(n_sub):
                lo, hi = c * LANES, (c + 1) * LANES
                parts = [jnp.full((lo, LANES), -jnp.inf, F32)] if c else []
                parts.append(jnp.where(tri, sd[lo:hi, lo:hi], -jnp.inf))
                if hi < tq:
                    parts.append(sd[hi:, lo:hi])
                cols.append(jnp.concatenate(parts, axis=0))
            best = functools.reduce(jnp.maximum, cols)
            m_next = jnp.broadcast_to(jnp.max(best, axis=1, keepdims=True), (tq, LANES))
            p_cols = []
            for c in range(n_sub):
                lo = c * LANES
                parts = [jnp.zeros((lo, LANES), F32)] if c else []
                parts.append(jnp.exp2(cols[c][lo:, :] - m_next[lo:, :]))
                p_cols.append(jnp.concatenate(parts, axis=0))
            ps.append(jnp.concatenate(p_cols, axis=1).astype(BF16))
            m[hh] = m_next
        acc[:half, :] = pv_of([p[:half, :half] for p in ps], k * tk, half)
        acc[half:, :] = pv_of([p[half:, :] for p in ps], k * tk, tk)

    def update(c, slot, m, acc, s):
        ps, alphas = [], []
        for hh in range(2):
            sc = s[slot, hh]
            m_prev = m[hh]
            m_next = jnp.maximum(m_prev, jnp.max(sc, axis=1, keepdims=True))
            alphas.append(jnp.exp2(m_prev - m_next))
            ps.append(jnp.concatenate(
                [jnp.exp2(sc[:, n * LANES:(n + 1) * LANES] - m_next) for n in range(n_sub)],
                axis=1).astype(BF16))
            m[hh] = m_next
        alpha = jnp.where(even_lanes, jnp.concatenate([alphas[0]] * 2, axis=1),
                          jnp.concatenate([alphas[1]] * 2, axis=1))
        acc[...] = alpha * acc[...] + pv_of(ps, c * tk, tk)

    def finish(acc, o):
        a = acc[...]
        l_e = jnp.broadcast_to(a[:, LANES:LANES + 1], (tq, LANES))
        l_o = jnp.broadcast_to(a[:, LANES + 1:LANES + 2], (tq, LANES))
        lane = lax.broadcasted_iota(jnp.int32, (tq, LANES), 1)
        o[...] = (a[:, :LANES] / jnp.where(lane < FOX_HEAD_DIM, l_e, l_o)).astype(o.dtype)

    def step(kk):
        if kk == 0:
            build_v()
        for sub in range(tiles):
            k = tiles * kk + sub
            rows = pl.ds(sub * tq, tq)
            q, o = q_ref.at[rows, :], o_ref.at[rows, :]
            m, acc, s = m_ref.at[sub], acc_ref.at[sub], s_ref.at[sub]
            last = sub == tiles - 1 and kk + 1 < n_steps

            def next_step_diagonal():
                logits_diagonal(tiles * (kk + 1), qn_ref.at[pl.ds(0, tq), :], s_ref.at[0])

            if sub > 0 or kk == 0:
                logits_diagonal(k, q, s)
            if k > 0:
                logits(k - 1, 1, q, s)
            update_diagonal(k, m, acc, s)
            slot = 1
            for c in range(k - 1, -1, -1):
                if c > 0:
                    logits(c - 1, 1 - slot, q, s)
                elif last:
                    next_step_diagonal()
                update(c, slot, m, acc, s)
                slot = 1 - slot
            if k == 0 and last:
                next_step_diagonal()
            finish(acc, o)

    for kk in range(n_steps):
        pl.when(i == kk)(functools.partial(step, kk))


def _attention(qa, ka, v, *, batch, seq, n_heads, tq):
    t = qa.shape[0]
    n_pairs = n_heads // 2
    nq = seq // tq
    tiles = 2 if nq % 2 == 0 else 1
    n_steps = nq // tiles
    rows = tiles * tq
    kern = functools.partial(_attn_kernel, tq=tq, n_steps=n_steps)
    return pl.pallas_call(
        kern,
        grid=(batch, n_pairs, n_steps),
        in_specs=[pl.BlockSpec((rows, 2 * LANES), lambda b, p, i: (b * n_steps + i, p)),
                  pl.BlockSpec((rows, 2 * LANES),
                               lambda b, p, i: (b * n_steps + jnp.minimum(i + 1, n_steps - 1), p)),
                  pl.BlockSpec((seq, 2 * LANES), lambda b, p, i: (b, p)),
                  pl.BlockSpec((seq, LANES), lambda b, p, i: (b, p))],
        out_specs=pl.BlockSpec((rows, LANES), lambda b, p, i: (b * n_steps + i, p)),
        out_shape=jax.ShapeDtypeStruct((t, n_pairs * LANES), BF16),
        scratch_shapes=[pltpu.VMEM((tiles, 2, tq, LANES), F32),
                        pltpu.VMEM((tiles, tq, 2 * LANES), F32),
                        pltpu.VMEM((seq, 2 * LANES), BF16), pltpu.VMEM((seq, 2 * LANES), BF16),
                        pltpu.VMEM((tiles, DIAG_SLOT + 1, 2, tq, tq), F32)],
        compiler_params=_params(3),
        name="fox_attn",
    )(qa, qa, ka, v)


def _mem_kv_kernel(mem_ref, g_ref, w_ref, kv_ref):
    h = _rms(mem_ref[...], g_ref[...]).astype(BF16)
    kv_ref[...] = jnp.dot(h, w_ref[...].astype(BF16), preferred_element_type=F32).astype(BF16)


def _mem_kv(mem2d, g, w_xkv, layer, *, batch):
    tmem = mem2d.shape[0] // batch
    d = mem2d.shape[1]
    n = w_xkv.shape[2]
    return pl.pallas_call(
        _mem_kv_kernel,
        grid=(batch,),
        in_specs=[pl.BlockSpec((tmem, d), lambda b: (b, 0)), _const_spec(g.shape),
                  _layer_spec(w_xkv, layer)],
        out_specs=pl.BlockSpec((tmem, n), lambda b: (b, 0)),
        out_shape=jax.ShapeDtypeStruct((mem2d.shape[0], n), BF16),
        compiler_params=_params(1),
        name="mem_kv",
    )(mem2d, g, w_xkv)


def _mix_kernel(x_ref, u_ref, up_ref, yf_ref, kv_ref, wp_ref, ps_ref, wm_ref,
                g_mpost_ref, g_xpre_ref, wq_ref, wo_ref, g_xpost_ref, wu_ref, wd_ref,
                o_ref, wu_bf_ref, wd_bf_ref, wbf_ref, *, tiles_per_seq):
    i = pl.program_id(0)
    tm, d = x_ref.shape
    seq_tile = i % tiles_per_seq

    wu_bf_ref[...] = wu_ref[...].astype(BF16)
    wd_bf_ref[...] = wd_ref[...].astype(BF16)

    @pl.when(i == 0)
    def _():
        for n, w_ref in enumerate((wm_ref, wq_ref, wo_ref)):
            wbf_ref[n] = w_ref[...].astype(BF16)

    halo = jnp.where(seq_tile == 0, 0.0, up_ref[...])
    ub = jnp.concatenate([halo, u_ref[...]], axis=0)
    s2 = ub + pltpu.roll(ub, 1, 0)
    s4 = s2 + pltpu.roll(s2, 2, 0)
    s8 = s4 + pltpu.roll(s4, 4, 0)
    s16 = s8 + pltpu.roll(s8, 8, 0)
    sums = dict(zip(POOL_WINDOWS, (s2, s4, s8, s16)))
    pos = seq_tile * tm + lax.broadcasted_iota(jnp.int32, (tm, D_POOL), 0) + 1
    lane = lax.broadcasted_iota(jnp.int32, (tm, D_POOL), 1)
    pooled = jnp.zeros((tm, D_POOL), F32)
    for g, w in enumerate(POOL_WINDOWS):
        cnt = jnp.minimum(pos, w).astype(F32)
        mean = sums[w][POOL_HALO:, :] / cnt
        in_group = (lane >= g * POOL_GROUP_DIM) & (lane < (g + 1) * POOL_GROUP_DIM)
        pooled = jnp.where(in_group, mean, pooled)
    diff = (pooled - u_ref[...]).astype(BF16)
    y_pool = jnp.dot(diff, wp_ref[...], preferred_element_type=F32) * ps_ref[...]

    y = jnp.dot(y_pool.astype(BF16), wbf_ref[0, :D_POOL, :], preferred_element_type=F32)
    y = y + jnp.dot(yf_ref[...], wbf_ref[0, D_POOL:, :], preferred_element_type=F32)
    x1 = x_ref[...] + _rms(y, g_mpost_ref[...])

    h = _rms(x1, g_xpre_ref[...]).astype(BF16)
    q = jnp.dot(h, wbf_ref[1], preferred_element_type=F32)
    dh = d // XA_HEADS
    outs = []
    for hd in range(XA_HEADS):
        qh = (q[:, hd * dh:(hd + 1) * dh] * (dh ** -0.5)).astype(BF16)
        kh = kv_ref[:, hd * dh:(hd + 1) * dh]
        vh = kv_ref[:, d + hd * dh:d + (hd + 1) * dh]
        s = lax.dot_general(qh, kh, (((1,), (1,)), ((), ())), preferred_element_type=F32)
        p = jnp.exp(s - jnp.max(s, axis=1, keepdims=True))
        o = jnp.dot(p.astype(BF16), vh, preferred_element_type=F32)
        outs.append((o / jnp.sum(p, axis=1, keepdims=True)).astype(BF16))
    att = jnp.concatenate(outs, axis=1)
    y2 = jnp.dot(att, wbf_ref[2], preferred_element_type=F32)
    o_ref[...] = x1 + _rms(y2, g_xpost_ref[...])


def _mix(x2d, u, yf, kv, w_pool_bd, pool_scale, w_mix_out, g_mpost, g_xpre,
         w_xq, w_xo, g_xpost, w_up, w_down, layer, *, seq, tm):
    t, d = x2d.shape
    tps = seq // tm
    hb = tm // POOL_HALO
    tmem = kv.shape[0] // (t // seq)
    steps = t // tm
    up_cols, down_rows = w_up.shape[2] // steps, w_down.shape[1] // steps
    assert up_cols * steps == w_up.shape[2] and up_cols % LANES == 0
    assert down_rows * steps == w_down.shape[1] and down_rows % BF16_SUBLANES == 0
    kern = functools.partial(_mix_kernel, tiles_per_seq=tps)
    row = lambda i: (i, 0)
    cs = _const_spec
    return pl.pallas_call(
        kern,
        grid=(t // tm,),
        in_specs=[pl.BlockSpec((tm, d), row), pl.BlockSpec((tm, D_POOL), row),
                  pl.BlockSpec((POOL_HALO, D_POOL), lambda i: (jnp.maximum(i * hb - 1, 0), 0)),
                  pl.BlockSpec((tm, yf.shape[1]), row),
                  pl.BlockSpec((tmem, kv.shape[1]), lambda i: (i // tps, 0)),
                  cs(w_pool_bd.shape), cs(pool_scale.shape), _layer_spec(w_mix_out, layer),
                  cs(g_mpost.shape), cs(g_xpre.shape), _layer_spec(w_xq, layer),
                  _layer_spec(w_xo, layer), cs(g_xpost.shape),
                  pl.BlockSpec((None, w_up.shape[1], up_cols), lambda i: (layer, 0, i)),
                  pl.BlockSpec((None, down_rows, w_down.shape[2]), lambda i: (layer, i, 0))],
        out_specs=[pl.BlockSpec((tm, d), row),
                   pl.BlockSpec((w_up.shape[1], up_cols), lambda i: (0, i)),
                   pl.BlockSpec((down_rows, w_down.shape[2]), lambda i: (i, 0))],
        out_shape=[jax.ShapeDtypeStruct((t, d), F32),
                   jax.ShapeDtypeStruct(w_up.shape[1:], BF16),
                   jax.ShapeDtypeStruct(w_down.shape[1:], BF16)],
        scratch_shapes=[pltpu.VMEM((3, d, d), BF16)],
        compiler_params=_params(1),
        name="mix_xattn",
    )(x2d, u, u, yf, kv, w_pool_bd, pool_scale, w_mix_out, g_mpost, g_xpre, w_xq, w_xo, g_xpost,
      w_up, w_down)


def _gated_gelu(gate, up):
    a = -2.0 * LOG2E * math.sqrt(2.0 / math.pi)
    e = jnp.exp2(gate * (a + (a * 0.044715) * (gate * gate)))
    return (gate * up) / (1.0 + e)


def _ffn_kernel(x_ref, g_pre_ref, wu_ref, cw_ref, cb_ref, wd_ref, g_post_ref, o_ref, carry_ref,
                *, tiles_per_seq, ck):
    i = pl.program_id(0)
    tm = x_ref.shape[0]
    d_ff = wd_ref.shape[0]

    @pl.when(i % tiles_per_seq == 0)
    def _():
        carry_ref[...] = jnp.zeros_like(carry_ref)

    h = _rms(x_ref[...], g_pre_ref[...]).astype(BF16)

    def taps(xs, lo):
        out = xs * cw_ref[CONV_WIDTH - 1:CONV_WIDTH, lo:lo + ck]
        for tap in range(1, CONV_WIDTH):
            out = out + pltpu.roll(xs, tap, 0) * cw_ref[CONV_WIDTH - 1 - tap:CONV_WIDTH - tap, lo:lo + ck]
        return out

    def conv(c, which):
        lo = which * d_ff + c * ck
        hid = jnp.dot(h, wu_ref[:, lo:lo + ck], preferred_element_type=F32)
        prev = carry_ref[c, which]
        carry_ref[c, which] = hid[tm - CONV_CARRY:, :]
        head = taps(jnp.concatenate([prev, hid[:CONV_CARRY, :]], axis=0), lo)[CONV_CARRY:, :]
        body = taps(hid, lo)[CONV_CARRY:, :]
        return jnp.concatenate([head, body], axis=0) + cb_ref[:, lo:lo + ck]

    acc = jnp.zeros((tm, x_ref.shape[1]), F32)
    for c in range(d_ff // ck):
        act = _gated_gelu(conv(c, 0), conv(c, 1)).astype(BF16)
        acc = acc + jnp.dot(act, wd_ref[c * ck:(c + 1) * ck, :], preferred_element_type=F32)
    o_ref[...] = x_ref[...] + _rms(acc, g_post_ref[...])


def _ffn(x2d, g_pre, w_up, conv_w, conv_b, w_down, g_post, *, seq, tm, ck):
    t, d = x2d.shape
    d_ff = w_down.shape[0]
    kern = functools.partial(_ffn_kernel, tiles_per_seq=seq // tm, ck=ck)
    row = lambda i: (i, 0)
    consts = (g_pre, w_up, conv_w, conv_b, w_down, g_post)
    return pl.pallas_call(
        kern,
        grid=(t // tm,),
        in_specs=[pl.BlockSpec((tm, d), row)] + [_const_spec(a.shape) for a in consts],
        out_specs=pl.BlockSpec((tm, d), row),
        out_shape=jax.ShapeDtypeStruct((t, d), F32),
        scratch_shapes=[pltpu.VMEM((d_ff // ck, 2, CONV_CARRY, ck), F32)],
        compiler_params=_params(1),
        name="conv_ffn",
    )(x2d, *consts)


def _tile(n, pref):
    t = min(n, pref)
    assert n % t == 0, (n, t)
    return t


def kernel(x, mem, norm_mix_pre, norm_mix_post, w_in, b_forget, w_pool, pool_scale, w_mix_out,
           norm_mem, norm_xa_pre, norm_xa_post, w_xq, w_xkv, w_xo,
           norm_ffn_pre, norm_ffn_post, w_up, conv_w, conv_b, w_down):
    b, s, d = x.shape
    depth = w_in.shape[0]
    n_heads = b_forget.shape[1]
    d_fox = n_heads * FOX_HEAD_DIM
    d_main = D_POOL + 3 * d_fox
    assert n_heads % 2 == 0 and n_heads <= LANES
    assert w_in.shape[2] == d_main + n_heads
    tm = _tile(s, 512)
    tm_in = _tile(s, 1024)
    tq = _tile(s, 512)
    tm_ffn = _tile(s, 512)
    ck = _tile(w_down.shape[1], 4096)

    x2d = x.reshape(b * s, d)
    mem2d = mem.reshape(b * mem.shape[1], d)
    w_in_t = jnp.swapaxes(w_in, 1, 2)
    for l in range(depth):
        row = lambda a: a[l].reshape(1, -1)
        b_f = jnp.pad(b_forget[l], (0, LANES - n_heads)).reshape(1, LANES)
        w_pool_bd = jax.scipy.linalg.block_diag(*w_pool[l]).astype(BF16)

        u, qa, ka, v = _in_proj(x2d, row(norm_mix_pre), w_in_t, l, b_f,
                                seq=s, n_heads=n_heads, tm=tm_in)
        yf = _attention(qa, ka, v, batch=b, seq=s, n_heads=n_heads, tq=tq)
        kv = _mem_kv(mem2d, row(norm_mem), w_xkv, l, batch=b)
        x2d, w_up_bf, w_down_bf = _mix(
            x2d, u, yf, kv, w_pool_bd, pool_scale[l].reshape(1, -1), w_mix_out,
            row(norm_mix_post), row(norm_xa_pre), w_xq, w_xo, row(norm_xa_post), w_up, w_down, l,
            seq=s, tm=tm)
        x2d = _ffn(x2d, row(norm_ffn_pre), w_up_bf, conv_w[l].reshape(CONV_WIDTH, -1),
                   conv_b[l].reshape(1, -1), w_down_bf, row(norm_ffn_post),
                   seq=s, tm=tm_ffn, ck=ck)
    return x2d.reshape(b, s, d)
```

```python
import functools
import math

import jax
import jax.numpy as jnp
import numpy as np
from jax import lax
from jax.experimental import pallas as pl
from jax.experimental.pallas import tpu as pltpu

F32 = jnp.float32
BF16 = jnp.bfloat16

NORM_EPS = 1e-6
POOL_WINDOWS = (2, 4, 8, 16)
POOL_GROUP_DIM = 64
D_POOL = 256
FOX_HEAD_DIM = 64
XA_HEADS = 4
CONV_WIDTH = 3
LANES = 128
POOL_HALO = 16
CONV_CARRY = 8
BF16_SUBLANES = 16
AUG = 3
VMEM_LIMIT = 56 * 1024 * 1024
LOG2E = math.log2(math.e)
DIAG_SLOT = 2


def _rms(x, g):
    ms = jnp.mean(x * x, axis=-1, keepdims=True)
    return x * lax.rsqrt(ms + NORM_EPS) * g


def _const_spec(shape):
    nd = len(shape)
    return pl.BlockSpec(shape, lambda *_: (0,) * nd, pipeline_mode=pl.Buffered(1))


def _layer_spec(w, layer):
    return pl.BlockSpec((None,) + w.shape[1:], lambda *_: (layer,) + (0,) * (w.ndim - 1),
                        pipeline_mode=pl.Buffered(1))


def _params(n_axes):
    return pltpu.CompilerParams(dimension_semantics=("arbitrary",) * n_axes,
                                vmem_limit_bytes=VMEM_LIMIT)


def _aug_constants(n_heads):
    place = np.zeros((AUG * LANES, 2 * LANES), np.float32)
    ones = np.zeros((8, 2 * LANES), np.float32)
    for h in range(n_heads):
        for n in range(AUG):
            place[n * LANES + h, 2 * AUG * h + n] = 1.0
            place[n * LANES + h, LANES + 2 * AUG * h + AUG + n] = -1.0
            ones[0, 2 * AUG * h + AUG + n] = 1.0
            ones[0, LANES + 2 * AUG * h + n] = 1.0
    return jnp.asarray(place, BF16), jnp.asarray(ones)


def _in_proj_kernel(x_ref, g_ref, w_ref, bf_ref, place_ref, ones_ref,
                    u_ref, qa_ref, ka_ref, v_ref, carry_ref, wm_ref, wf_ref,
                    *, tiles_per_seq, n_heads):
    i = pl.program_id(0)
    tm = x_ref.shape[0]
    d_fox = n_heads * FOX_HEAD_DIM
    d_main = D_POOL + 3 * d_fox

    @pl.when(i == 0)
    def _():
        blk = 2 * LANES
        for lo in range(0, d_main, blk):
            wm_ref[:, lo:lo + blk] = w_ref[lo:lo + blk, :].T.astype(BF16)
        wf_ref[...] = jnp.zeros_like(wf_ref)
        wf_ref[:n_heads, :] = w_ref[d_main:d_main + n_heads, :].astype(BF16)

    @pl.when(i % tiles_per_seq == 0)
    def _():
        carry_ref[...] = jnp.zeros_like(carry_ref)

    h = _rms(x_ref[...], g_ref[...]).astype(BF16)

    f = lax.dot_general(h, wf_ref[...], (((1,), (1,)), ((), ())),
                        preferred_element_type=F32) + bf_ref[...]
    c = jnp.minimum(f, 0.0) - jnp.log1p(jnp.exp(-jnp.abs(f)))
    row = lax.broadcasted_iota(jnp.int32, c.shape, 0)
    d = 1
    while d < tm:
        c = c + jnp.where(row >= d, pltpu.roll(c, d, 0), 0.0)
        d *= 2
    c = c + carry_ref[...]
    carry_ref[...] = c[tm - 1:tm, :]

    c2 = c * LOG2E
    hi = c2.astype(BF16)
    r1 = c2 - hi.astype(F32)
    mid = r1.astype(BF16)
    lo = (r1 - mid.astype(F32)).astype(BF16)
    pieces = jnp.concatenate([hi, mid, lo], axis=1)
    aug = jnp.dot(pieces, place_ref[...], preferred_element_type=F32) + ones_ref[0:1, :]
    aug_q, aug_k = aug[:, :LANES], aug[:, LANES:]

    main = jnp.dot(h, wm_ref[...], preferred_element_type=F32)
    u_ref[...] = main[:, :D_POOL]
    q = main[:, D_POOL:D_POOL + d_fox] * (FOX_HEAD_DIM ** -0.5 * LOG2E)
    k = main[:, D_POOL + d_fox:D_POOL + 2 * d_fox]
    v_ref[...] = main[:, D_POOL + 2 * d_fox:D_POOL + 3 * d_fox].astype(BF16)

    lane = lax.broadcasted_iota(jnp.int32, (tm, LANES), 1)
    for hd in range(n_heads):
        pair, odd = divmod(hd, 2)
        qk_lo = odd * FOX_HEAD_DIM
        a0 = (1 - odd) * FOX_HEAD_DIM
        shift = (a0 - 2 * AUG * hd) % LANES
        in_head = (lane >= qk_lo) & (lane < qk_lo + FOX_HEAD_DIM)
        in_aug = (lane >= a0) & (lane < a0 + 2 * AUG)
        sl = slice(pair * LANES, (pair + 1) * LANES)
        osl = slice(hd * LANES, (hd + 1) * LANES)
        rq = jnp.where(in_aug, pltpu.roll(aug_q, shift, 1), 0.0)
        rk = jnp.where(in_aug, pltpu.roll(aug_k, shift, 1), 0.0)
        qa_ref[:, osl] = jnp.where(in_head, q[:, sl], rq).astype(BF16)
        ka_ref[:, osl] = jnp.where(in_head, k[:, sl], rk).astype(BF16)


def _in_proj(x2d, g, w_in_t, layer, b_f, *, seq, n_heads, tm):
    t, d = x2d.shape
    d_fox = n_heads * FOX_HEAD_DIM
    d_main = D_POOL + 3 * d_fox
    place, ones = _aug_constants(n_heads)
    kern = functools.partial(_in_proj_kernel, tiles_per_seq=seq // tm, n_heads=n_heads)
    row = lambda i: (i, 0)
    return pl.pallas_call(
        kern,
        grid=(t // tm,),
        in_specs=[pl.BlockSpec((tm, d), row), _const_spec(g.shape), _layer_spec(w_in_t, layer),
                  _const_spec(b_f.shape), _const_spec(place.shape), _const_spec(ones.shape)],
        out_specs=[pl.BlockSpec((tm, D_POOL), row), pl.BlockSpec((tm, n_heads * LANES), row),
                   pl.BlockSpec((tm, n_heads * LANES), row), pl.BlockSpec((tm, d_fox), row)],
        out_shape=[jax.ShapeDtypeStruct((t, D_POOL), F32),
                   jax.ShapeDtypeStruct((t, n_heads * LANES), BF16),
                   jax.ShapeDtypeStruct((t, n_heads * LANES), BF16),
                   jax.ShapeDtypeStruct((t, d_fox), BF16)],
        scratch_shapes=[pltpu.VMEM((1, LANES), F32), pltpu.VMEM((d, d_main), BF16),
                        pltpu.VMEM((LANES, d), BF16)],
        compiler_params=_params(1),
        name="in_proj",
    )(x2d, g, w_in_t, b_f, place, ones)


def _attn_kernel(q_ref, qn_ref, k_ref, v_ref, o_ref, m_ref, acc_ref, vtop_ref, vbot_ref, s_ref,
                 *, tq, n_steps):
    i = pl.program_id(2)
    tk = tq
    tiles = q_ref.shape[0] // tq
    n_sub = tk // LANES
    half = tq // 2
    nt = (((1,), (1,)), ((), ()))

    def build_v():
        v = v_ref[...].astype(F32)
        lane = lax.broadcasted_iota(jnp.int32, v.shape, 1)
        vtop_ref[:, :LANES] = jnp.where(lane < FOX_HEAD_DIM, v, 0.0).astype(BF16)
        vtop_ref[:, LANES:] = jnp.where(lane == 0, 1.0, 0.0).astype(BF16)
        vbot_ref[:, :LANES] = jnp.where(lane >= FOX_HEAD_DIM, v, 0.0).astype(BF16)
        vbot_ref[:, LANES:] = jnp.where(lane == 1, 1.0, 0.0).astype(BF16)

    lane2 = lax.broadcasted_iota(jnp.int32, (tq, 2 * LANES), 1)
    even_lanes = (lane2 < FOX_HEAD_DIM) | (lane2 == LANES)

    def pv_of(ps, key0, n_keys):
        pv = jnp.dot(ps[0], vtop_ref[pl.ds(key0, n_keys), :], preferred_element_type=F32)
        return pv + jnp.dot(ps[1], vbot_ref[pl.ds(key0, n_keys), :], preferred_element_type=F32)

    def logits_diagonal(k, q, s):
        for hh in range(2):
            qh = q[:, hh * LANES:(hh + 1) * LANES]
            kh = k_ref[pl.ds(k * tk, tk), hh * LANES:(hh + 1) * LANES]
            s[DIAG_SLOT, hh, :half, :half] = lax.dot_general(qh[:half], kh[:half], nt,
                                                            preferred_element_type=F32)
            s[DIAG_SLOT, hh, half:, :] = lax.dot_general(qh[half:], kh, nt,
                                                         preferred_element_type=F32)

    def logits(c, slot, q, s):
        for hh in range(2):
            qh = q[:, hh * LANES:(hh + 1) * LANES]
            kh = k_ref[pl.ds(c * tk, tk), hh * LANES:(hh + 1) * LANES]
            s[slot, hh] = lax.dot_general(qh, kh, nt, preferred_element_type=F32)

    def update_diagonal(k, m, acc, s):
        tri = (lax.broadcasted_iota(jnp.int32, (LANES, LANES), 0)
               >= lax.broadcasted_iota(jnp.int32, (LANES, LANES), 1))
        ps = []
        for hh in range(2):
            sd = s.at[DIAG_SLOT, hh]
            cols = []
            for c in range(n_sub):
                lo, hi = c * LANES, (c + 1) * LANES
                parts = [jnp.full((lo, LANES), -jnp.inf, F32)] if c else []
                parts.append(jnp.where(tri, sd[lo:hi, lo:hi], -jnp.inf))
                if hi < tq:
                    parts.append(sd[hi:, lo:hi])
                cols.append(jnp.concatenate(parts, axis=0))
            best = functools.reduce(jnp.maximum, cols)
            m_next = jnp.broadcast_to(jnp.max(best, axis=1, keepdims=True), (tq, LANES))
            p_cols = []
            for c in range(n_sub):
                lo = c * LANES
                parts = [jnp.zeros((lo, LANES), F32)] if c else []
                parts.append(jnp.exp2(cols[c][lo:, :] - m_next[lo:, :]))
                p_cols.append(jnp.concatenate(parts, axis=0))
            ps.append(jnp.concatenate(p_cols, axis=1).astype(BF16))
            m[hh] = m_next
        acc[:half, :] = pv_of([p[:half, :half] for p in ps], k * tk, half)
        acc[half:, :] = pv_of([p[half:, :] for p in ps], k * tk, tk)

    def update(c, slot, m, acc, s):
        ps, alphas = [], []
        for hh in range(2):
            sc = s[slot, hh]
            m_prev = m[hh]
            m_next = jnp.maximum(m_prev, jnp.max(sc, axis=1, keepdims=True))
            alphas.append(jnp.exp2(m_prev - m_next))
            ps.append(jnp.concatenate(
                [jnp.exp2(sc[:, n * LANES:(n + 1) * LANES] - m_next) for n in range(n_sub)],
                axis=1).astype(BF16))
            m[hh] = m_next
        alpha = jnp.where(even_lanes, jnp.concatenate([alphas[0]] * 2, axis=1),
                          jnp.concatenate([alphas[1]] * 2, axis=1))
        acc[...] = alpha * acc[...] + pv_of(ps, c * tk, tk)

    def finish(acc, o):
        a = acc[...]
        l_e = jnp.broadcast_to(a[:, LANES:LANES + 1], (tq, LANES))
        l_o = jnp.broadcast_to(a[:, LANES + 1:LANES + 2], (tq, LANES))
        lane = lax.broadcasted_iota(jnp.int32, (tq, LANES), 1)
        o[...] = (a[:, :LANES] / jnp.where(lane < FOX_HEAD_DIM, l_e, l_o)).astype(o.dtype)

    def step(kk):
        if kk == 0:
            build_v()
        for sub in range(tiles):
            k = tiles * kk + sub
            rows = pl.ds(sub * tq, tq)
            q, o = q_ref.at[rows, :], o_ref.at[rows, :]
            m, acc, s = m_ref.at[sub], acc_ref.at[sub], s_ref.at[sub]
            last = sub == tiles - 1 and kk + 1 < n_steps

            def next_step_diagonal():
                logits_diagonal(tiles * (kk + 1), qn_ref.at[pl.ds(0, tq), :], s_ref.at[0])

            if sub > 0 or kk == 0:
                logits_diagonal(k, q, s)
            if k > 0:
                logits(k - 1, 1, q, s)
            update_diagonal(k, m, acc, s)
            slot = 1
            for c in range(k - 1, -1, -1):
                if c > 0:
                    logits(c - 1, 1 - slot, q, s)
                elif last:
                    next_step_diagonal()
                update(c, slot, m, acc, s)
                slot = 1 - slot
            if k == 0 and last:
                next_step_diagonal()
            finish(acc, o)

    for kk in range(n_steps):
        pl.when(i == kk)(functools.partial(step, kk))


def _attention(qa, ka, v, *, batch, seq, n_heads, tq):
    t = qa.shape[0]
    n_pairs = n_heads // 2
    nq = seq // tq
    tiles = next(n for n in (4, 2, 1) if nq % n == 0)
    n_steps = nq // tiles
    rows = tiles * tq
    kern = functools.partial(_attn_kernel, tq=tq, n_steps=n_steps)
    return pl.pallas_call(
        kern,
        grid=(batch, n_pairs, n_steps),
        in_specs=[pl.BlockSpec((rows, 2 * LANES), lambda b, p, i: (b * n_steps + i, p)),
                  pl.BlockSpec((rows, 2 * LANES),
                               lambda b, p, i: (b * n_steps + jnp.minimum(i + 1, n_steps - 1), p)),
                  pl.BlockSpec((seq, 2 * LANES), lambda b, p, i: (b, p)),
                  pl.BlockSpec((seq, LANES), lambda b, p, i: (b, p))],
        out_specs=pl.BlockSpec((rows, LANES), lambda b, p, i: (b * n_steps + i, p)),
        out_shape=jax.ShapeDtypeStruct((t, n_pairs * LANES), BF16),
        scratch_shapes=[pltpu.VMEM((tiles, 2, tq, LANES), F32),
                        pltpu.VMEM((tiles, tq, 2 * LANES), F32),
                        pltpu.VMEM((seq, 2 * LANES), BF16), pltpu.VMEM((seq, 2 * LANES), BF16),
                        pltpu.VMEM((tiles, DIAG_SLOT + 1, 2, tq, tq), F32)],
        compiler_params=_params(3),
        name="fox_attn",
    )(qa, qa, ka, v)


def _mem_kv_kernel(mem_ref, g_ref, w_ref, kv_ref):
    h = _rms(mem_ref[...], g_ref[...]).astype(BF16)
    kv_ref[...] = jnp.dot(h, w_ref[...].astype(BF16), preferred_element_type=F32).astype(BF16)


def _mem_kv(mem2d, g, w_xkv, layer, *, batch):
    tmem = mem2d.shape[0] // batch
    d = mem2d.shape[1]
    n = w_xkv.shape[2]
    return pl.pallas_call(
        _mem_kv_kernel,
        grid=(batch,),
        in_specs=[pl.BlockSpec((tmem, d), lambda b: (b, 0)), _const_spec(g.shape),
                  _layer_spec(w_xkv, layer)],
        out_specs=pl.BlockSpec((tmem, n), lambda b: (b, 0)),
        out_shape=jax.ShapeDtypeStruct((mem2d.shape[0], n), BF16),
        compiler_params=_params(1),
        name="mem_kv",
    )(mem2d, g, w_xkv)


def _mix_kernel(x_ref, u_ref, up_ref, yf_ref, kv_ref, wp_ref, ps_ref, wm_ref,
                g_mpost_ref, g_xpre_ref, wq_ref, wo_ref, g_xpost_ref, wu_ref, wd_ref,
                o_ref, wu_bf_ref, wd_bf_ref, wbf_ref, *, tiles_per_seq):
    i = pl.program_id(0)
    tm, d = x_ref.shape
    seq_tile = i % tiles_per_seq

    wu_bf_ref[...] = wu_ref[...].astype(BF16)
    wd_bf_ref[...] = wd_ref[...].astype(BF16)

    @pl.when(i == 0)
    def _():
        for n, w_ref in enumerate((wm_ref, wq_ref, wo_ref)):
            wbf_ref[n] = w_ref[...].astype(BF16)

    halo = jnp.where(seq_tile == 0, 0.0, up_ref[...])
    ub = jnp.concatenate([halo, u_ref[...]], axis=0)
    s2 = ub + pltpu.roll(ub, 1, 0)
    s4 = s2 + pltpu.roll(s2, 2, 0)
    s8 = s4 + pltpu.roll(s4, 4, 0)
    s16 = s8 + pltpu.roll(s8, 8, 0)
    sums = dict(zip(POOL_WINDOWS, (s2, s4, s8, s16)))
    pos = seq_tile * tm + lax.broadcasted_iota(jnp.int32, (tm, D_POOL), 0) + 1
    lane = lax.broadcasted_iota(jnp.int32, (tm, D_POOL), 1)
    pooled = jnp.zeros((tm, D_POOL), F32)
    for g, w in enumerate(POOL_WINDOWS):
        cnt = jnp.minimum(pos, w).astype(F32)
        mean = sums[w][POOL_HALO:, :] / cnt
        in_group = (lane >= g * POOL_GROUP_DIM) & (lane < (g + 1) * POOL_GROUP_DIM)
        pooled = jnp.where(in_group, mean, pooled)
    diff = (pooled - u_ref[...]).astype(BF16)
    y_pool = jnp.dot(diff, wp_ref[...], preferred_element_type=F32) * ps_ref[...]

    y = jnp.dot(y_pool.astype(BF16), wbf_ref[0, :D_POOL, :], preferred_element_type=F32)
    y = y + jnp.dot(yf_ref[...], wbf_ref[0, D_POOL:, :], preferred_element_type=F32)
    x1 = x_ref[...] + _rms(y, g_mpost_ref[...])

    h = _rms(x1, g_xpre_ref[...]).astype(BF16)
    q = jnp.dot(h, wbf_ref[1], preferred_element_type=F32)
    dh = d // XA_HEADS
    outs = []
    for hd in range(XA_HEADS):
        qh = (q[:, hd * dh:(hd + 1) * dh] * (dh ** -0.5)).astype(BF16)
        kh = kv_ref[:, hd * dh:(hd + 1) * dh]
        vh = kv_ref[:, d + hd * dh:d + (hd + 1) * dh]
        s = lax.dot_general(qh, kh, (((1,), (1,)), ((), ())), preferred_element_type=F32)
        p = jnp.exp(s - jnp.max(s, axis=1, keepdims=True))
        o = jnp.dot(p.astype(BF16), vh, preferred_element_type=F32)
        outs.append((o / jnp.sum(p, axis=1, keepdims=True)).astype(BF16))
    att = jnp.concatenate(outs, axis=1)
    y2 = jnp.dot(att, wbf_ref[2], preferred_element_type=F32)
    o_ref[...] = x1 + _rms(y2, g_xpost_ref[...])


def _mix(x2d, u, yf, kv, w_pool_bd, pool_scale, w_mix_out, g_mpost, g_xpre,
         w_xq, w_xo, g_xpost, w_up, w_down, layer, *, seq, tm):
    t, d = x2d.shape
    tps = seq // tm
    hb = tm // POOL_HALO
    tmem = kv.shape[0] // (t // seq)
    steps = t // tm
    up_cols, down_rows = w_up.shape[2] // steps, w_down.shape[1] // steps
    assert up_cols * steps == w_up.shape[2] and up_cols % LANES == 0
    assert down_rows * steps == w_down.shape[1] and down_rows % BF16_SUBLANES == 0
    kern = functools.partial(_mix_kernel, tiles_per_seq=tps)
    row = lambda i: (i, 0)
    cs = _const_spec
    return pl.pallas_call(
        kern,
        grid=(t // tm,),
        in_specs=[pl.BlockSpec((tm, d), row), pl.BlockSpec((tm, D_POOL), row),
                  pl.BlockSpec((POOL_HALO, D_POOL), lambda i: (jnp.maximum(i * hb - 1, 0), 0)),
                  pl.BlockSpec((tm, yf.shape[1]), row),
                  pl.BlockSpec((tmem, kv.shape[1]), lambda i: (i // tps, 0)),
                  cs(w_pool_bd.shape), cs(pool_scale.shape), _layer_spec(w_mix_out, layer),
                  cs(g_mpost.shape), cs(g_xpre.shape), _layer_spec(w_xq, layer),
                  _layer_spec(w_xo, layer), cs(g_xpost.shape),
                  pl.BlockSpec((None, w_up.shape[1], up_cols), lambda i: (layer, 0, i)),
                  pl.BlockSpec((None, down_rows, w_down.shape[2]), lambda i: (layer, i, 0))],
        out_specs=[pl.BlockSpec((tm, d), row),
                   pl.BlockSpec((w_up.shape[1], up_cols), lambda i: (0, i)),
                   pl.BlockSpec((down_rows, w_down.shape[2]), lambda i: (i, 0))],
        out_shape=[jax.ShapeDtypeStruct((t, d), F32),
                   jax.ShapeDtypeStruct(w_up.shape[1:], BF16),
                   jax.ShapeDtypeStruct(w_down.shape[1:], BF16)],
        scratch_shapes=[pltpu.VMEM((3, d, d), BF16)],
        compiler_params=_params(1),
        name="mix_xattn",
    )(x2d, u, u, yf, kv, w_pool_bd, pool_scale, w_mix_out, g_mpost, g_xpre, w_xq, w_xo, g_xpost,
      w_up, w_down)


def _gated_gelu(gate, up):
    a = -2.0 * LOG2E * math.sqrt(2.0 / math.pi)
    e = jnp.exp2(gate * (a + (a * 0.044715) * (gate * gate)))
    return (gate * up) / (1.0 + e)


def _ffn_kernel(x_ref, g_pre_ref, wu_ref, cw_ref, cb_ref, wd_ref, g_post_ref, o_ref, carry_ref,
                *, tiles_per_seq, ck):
    i = pl.program_id(0)
    tm = x_ref.shape[0]
    d_ff = wd_ref.shape[0]

    @pl.when(i % tiles_per_seq == 0)
    def _():
        carry_ref[...] = jnp.zeros_like(carry_ref)

    h = _rms(x_ref[...], g_pre_ref[...]).astype(BF16)

    def taps(xs, lo):
        out = xs * cw_ref[CONV_WIDTH - 1:CONV_WIDTH, lo:lo + ck]
        for tap in range(1, CONV_WIDTH):
            out = out + pltpu.roll(xs, tap, 0) * cw_ref[CONV_WIDTH - 1 - tap:CONV_WIDTH - tap, lo:lo + ck]
        return out

    def conv(c, which):
        lo = which * d_ff + c * ck
        hid = jnp.dot(h, wu_ref[:, lo:lo + ck], preferred_element_type=F32)
        prev = carry_ref[c, which]
        carry_ref[c, which] = hid[tm - CONV_CARRY:, :]
        head = taps(jnp.concatenate([prev, hid[:CONV_CARRY, :]], axis=0), lo)[CONV_CARRY:, :]
        body = taps(hid, lo)[CONV_CARRY:, :]
        return jnp.concatenate([head, body], axis=0) + cb_ref[:, lo:lo + ck]

    acc = jnp.zeros((tm, x_ref.shape[1]), F32)
    for c in range(d_ff // ck):
        act = _gated_gelu(conv(c, 0), conv(c, 1)).astype(BF16)
        acc = acc + jnp.dot(act, wd_ref[c * ck:(c + 1) * ck, :], preferred_element_type=F32)
    o_ref[...] = x_ref[...] + _rms(acc, g_post_ref[...])


def _ffn(x2d, g_pre, w_up, conv_w, conv_b, w_down, g_post, *, seq, tm, ck):
    t, d = x2d.shape
    d_ff = w_down.shape[0]
    kern = functools.partial(_ffn_kernel, tiles_per_seq=seq // tm, ck=ck)
    row = lambda i: (i, 0)
    consts = (g_pre, w_up, conv_w, conv_b, w_down, g_post)
    return pl.pallas_call(
        kern,
        grid=(t // tm,),
        in_specs=[pl.BlockSpec((tm, d), row)] + [_const_spec(a.shape) for a in consts],
        out_specs=pl.BlockSpec((tm, d), row),
        out_shape=jax.ShapeDtypeStruct((t, d), F32),
        scratch_shapes=[pltpu.VMEM((d_ff // ck, 2, CONV_CARRY, ck), F32)],
        compiler_params=_params(1),
        name="conv_ffn",
    )(x2d, *consts)


def _tile(n, pref):
    t = min(n, pref)
    assert n % t == 0, (n, t)
    return t


def kernel(x, mem, norm_mix_pre, norm_mix_post, w_in, b_forget, w_pool, pool_scale, w_mix_out,
           norm_mem, norm_xa_pre, norm_xa_post, w_xq, w_xkv, w_xo,
           norm_ffn_pre, norm_ffn_post, w_up, conv_w, conv_b, w_down):
    b, s, d = x.shape
    depth = w_in.shape[0]
    n_heads = b_forget.shape[1]
    d_fox = n_heads * FOX_HEAD_DIM
    d_main = D_POOL + 3 * d_fox
    assert n_heads % 2 == 0 and n_heads <= LANES
    assert w_in.shape[2] == d_main + n_heads
    tm = _tile(s, 512)
    tm_in = _tile(s, 1024)
    tq = _tile(s, 512)
    tm_ffn = _tile(s, 512)
    ck = _tile(w_down.shape[1], 4096)

    x2d = x.reshape(b * s, d)
    mem2d = mem.reshape(b * mem.shape[1], d)
    w_in_t = jnp.swapaxes(w_in, 1, 2)
    for l in range(depth):
        row = lambda a: a[l].reshape(1, -1)
        b_f = jnp.pad(b_forget[l], (0, LANES - n_heads)).reshape(1, LANES)
        w_pool_bd = jax.scipy.linalg.block_diag(*w_pool[l]).astype(BF16)

        u, qa, ka, v = _in_proj(x2d, row(norm_mix_pre), w_in_t, l, b_f,
                                seq=s, n_heads=n_heads, tm=tm_in)
        yf = _attention(qa, ka, v, batch=b, seq=s, n_heads=n_heads, tq=tq)
        kv = _mem_kv(mem2d, row(norm_mem), w_xkv, l, batch=b)
        x2d, w_up_bf, w_down_bf = _mix(
            x2d, u, yf, kv, w_pool_bd, pool_scale[l].reshape(1, -1), w_mix_out,
            row(norm_mix_post), row(norm_xa_pre), w_xq, w_xo, row(norm_xa_post), w_up, w_down, l,
            seq=s, tm=tm)
        x2d = _ffn(x2d, row(norm_ffn_pre), w_up_bf, conv_w[l].reshape(CONV_WIDTH, -1),
                   conv_b[l].reshape(1, -1), w_down_bf, row(norm_ffn_post),
                   seq=s, tm=tm_ffn, ck=ck)
    return x2d.reshape(b, s, d)
```
